```python
import math
import functools
import numpy as np
import jax
import jax.numpy as jnp
from jax import lax

D_MODEL = 2048
BATCH = 8
SEQ = 2048
DEPTH = 1
DEC_BATCH = 128
DEC_SEQ = 8
PAST_LEN = 2048
PAGE_SIZE = 128

N_HEADS = 16
HEAD_DIM = D_MODEL // N_HEADS
KV_HEADS = 4
Q_PER_KV = N_HEADS // KV_HEADS
IDX_HEADS = 16
IDX_DIM = 64
TOPK_MAX = 256
Q_BLOCK = 64
N_BUCKETS = 32
MAX_DISTANCE = 128
HG_EXPAND = 128
HG_HEADS = D_MODEL // HG_EXPAND
HG_DK = HG_EXPAND
HG_DV = D_MODEL // HG_HEADS
HG_CHUNK = 32
MEM_LEN = 256
X_HEADS = 4
X_HEAD_DIM = D_MODEL // X_HEADS
D_FF = 11 * D_MODEL // 4
CONV_W = 3
EPS = 1e-6

A_WIDTH = N_HEADS * HEAD_DIM
KV_WIDTH = KV_HEADS * HEAD_DIM
B_WIDTH = HG_HEADS * HG_DV
HG_KWIDTH = HG_HEADS * HG_DK
MERGE_WIDTH = A_WIDTH
MIX_SIZES = (A_WIDTH, KV_WIDTH, KV_WIDTH, IDX_HEADS * IDX_DIM, IDX_HEADS, IDX_DIM,
             HG_KWIDTH, HG_KWIDTH, B_WIDTH, B_WIDTH, MERGE_WIDTH, MERGE_WIDTH)
MIX_WIDTH = sum(MIX_SIZES)

kernel_name = 'hybrid_dsa_hgrn2_memxattn_convffn_step'


def rms_norm(x, g):
    xf = x.astype(jnp.float32)
    y = xf * lax.rsqrt(jnp.mean(xf * xf, axis=-1, keepdims=True) + EPS)
    return (y * g.astype(jnp.float32)).astype(x.dtype)


def rel_bucket(dist):
    max_exact = N_BUCKETS // 2
    d = jnp.maximum(dist, 0)
    log_ratio = jnp.log(jnp.maximum(d, 1).astype(jnp.float32) / max_exact) / math.log(MAX_DISTANCE / max_exact)
    large = jnp.minimum(max_exact + (log_ratio * (N_BUCKETS - max_exact)).astype(jnp.int32), N_BUCKETS - 1)
    return jnp.where(d < max_exact, d, large)


def indexer_scores(iq, iw, ik):
    qk = jnp.einsum('bthd,bsd->bths', iq, ik, preferred_element_type=jnp.float32) * (IDX_DIM ** -0.5)
    return jnp.einsum('bths,bth->bts', jax.nn.relu(qk), iw.astype(jnp.float32))


def sparse_attend(q, k_sel, v_sel, dist, rel_bias):
    B, T = q.shape[:2]
    qg = q.reshape(B, T, KV_HEADS, Q_PER_KV, HEAD_DIM)
    s = jnp.einsum('bthgd,btnhd->bthgn', qg, k_sel, preferred_element_type=jnp.float32) * (HEAD_DIM ** -0.5)
    bias = rel_bias.astype(jnp.float32)[rel_bucket(dist)]
    bias = jnp.moveaxis(bias.reshape(B, T, -1, KV_HEADS, Q_PER_KV), 2, -1)
    s = jnp.where((dist >= 0)[:, :, None, None, :], s + bias, -jnp.inf)
    p = jax.nn.softmax(s, axis=-1)
    o = jnp.einsum('bthgn,btnhd->bthgd', p.astype(v_sel.dtype), v_sel)
    return o.reshape(B, T, A_WIDTH)


def dsa_prompt(q, k, v, iq, iw, ik, rel_bias):
    B, S = q.shape[:2]
    topk = min(TOPK_MAX, S // 4)
    nblk = S // Q_BLOCK
    key_pos = jnp.arange(S)
    bidx = jnp.arange(B)[:, None, None]

    def to_blocks(a):
        return jnp.swapaxes(a.reshape((B, nblk, Q_BLOCK) + a.shape[2:]), 0, 1)

    def block(args):
        blk, qb, iqb, iwb = args
        q_pos = blk * Q_BLOCK + jnp.arange(Q_BLOCK)
        sc = indexer_scores(iqb, iwb, ik)
        sc = jnp.where((key_pos[None, :] <= q_pos[:, None])[None], sc, -jnp.inf)
        _, sel = lax.top_k(sc, topk)
        return sparse_attend(qb, k[bidx, sel], v[bidx, sel], q_pos[None, :, None] - sel, rel_bias)

    out = lax.map(block, (jnp.arange(nblk), to_blocks(q), to_blocks(iq), to_blocks(iw)))
    return jnp.swapaxes(out, 0, 1).reshape(B, S, A_WIDTH)


def dsa_sample(q, k, v, iq, iw, ik, cache_k, cache_v, cache_idx_k, page_table, rel_bias):
    B, T = q.shape[:2]
    page = cache_k.shape[1]
    past = page_table.shape[1] * page
    L = past + T
    topk = min(TOPK_MAX, L // 4)
    ik_past = cache_idx_k[page_table].reshape(B, past, IDX_DIM)
    ik_all = jnp.concatenate([ik_past, ik.astype(ik_past.dtype)], axis=1)
    q_pos = past + jnp.arange(T)
    sc = indexer_scores(iq, iw, ik_all)
    sc = jnp.where(jnp.arange(L)[None, None, :] <= q_pos[None, :, None], sc, -jnp.inf)
    _, sel = lax.top_k(sc, topk)
    bidx = jnp.arange(B)[:, None, None]
    sel_past = jnp.minimum(sel, past - 1)
    phys = page_table[bidx, sel_past // page]
    off = sel_past % page
    sel_new = jnp.clip(sel - past, 0, T - 1)
    in_past = (sel < past)[..., None, None]
    k_sel = jnp.where(in_past, cache_k[phys, off], k[bidx, sel_new].astype(cache_k.dtype))
    v_sel = jnp.where(in_past, cache_v[phys, off], v[bidx, sel_new].astype(cache_v.dtype))
    return sparse_attend(q, k_sel, v_sel, q_pos[None, :, None] - sel, rel_bias)


def hgrn_lower_bounds(hg_lb):
    sm = jax.nn.softmax(hg_lb.astype(jnp.float32), axis=0)
    return jnp.cumsum(sm, axis=0)[1:] - sm[0]


def hgrn_chunk(S0, q, k, v, logf):
    C = q.shape[1]
    b = jnp.cumsum(logf, axis=1)
    causal = jnp.tril(jnp.ones((C, C), dtype=bool))[None, :, :, None, None]
    decay = jnp.exp(jnp.where(causal, b[:, :, None] - b[:, None, :], -jnp.inf))
    a = jnp.sum(q[:, :, None] * k[:, None, :] * decay, axis=-1)
    o = jnp.einsum('btsh,bshv->bthv', a, v) + jnp.einsum('bthc,bhcv->bthv', q * jnp.exp(b), S0)
    b_end = b[:, -1]
    S = jnp.exp(b_end)[..., None] * S0 + jnp.einsum('bshc,bshv->bhcv', k * jnp.exp(b_end[:, None] - b), v)
    return S, o


def hgrn_prompt(q, k, v, logf):
    B, S = q.shape[:2]
    C = math.gcd(HG_CHUNK, S)
    n = S // C

    def split(a):
        return jnp.swapaxes(a.reshape((B, n, C) + a.shape[2:]), 0, 1)

    S0 = jnp.zeros((B, HG_HEADS, HG_DK, HG_DV), jnp.float32)
    S_fin, o = lax.scan(lambda st, xs: hgrn_chunk(st, *xs), S0, (split(q), split(k), split(v), split(logf)))
    return jnp.swapaxes(o, 0, 1).reshape(B, S, HG_HEADS, HG_DV), S_fin


def hgrn_sample(q, k, v, logf, S0):
    S, o = hgrn_chunk(S0.astype(jnp.float32), q, k, v, logf)
    return o, S


def mem_kv(mem, g_mem, w_xk, w_xv):
    B, M, _ = mem.shape
    m = rms_norm(mem, g_mem)
    mk = jnp.einsum('bmd,de->bme', m, w_xk).reshape(B, M, X_HEADS, X_HEAD_DIM)
    mv = jnp.einsum('bmd,de->bme', m, w_xv).reshape(B, M, X_HEADS, X_HEAD_DIM)
    return mk, mv


def cross_attend(h, w_xq, mem_k, mem_v, w_xo):
    B, T, _ = h.shape
    q = jnp.einsum('btd,de->bte', h, w_xq).reshape(B, T, X_HEADS, X_HEAD_DIM)
    s = jnp.einsum('bthd,bmhd->bhtm', q, mem_k, preferred_element_type=jnp.float32) * (X_HEAD_DIM ** -0.5)
    p = jax.nn.softmax(s, axis=-1)
    o = jnp.einsum('bhtm,bmhd->bthd', p.astype(mem_v.dtype), mem_v).reshape(B, T, D_MODEL)
    return jnp.einsum('bte,ed->btd', o, w_xo)


def conv_ffn(h, buf, w_up, conv_w, conv_b, w_down):
    T = h.shape[1]
    a, b = jnp.split(jnp.einsum('btd,de->bte', h, w_up), 2, axis=-1)
    a_ext = jnp.concatenate([buf.astype(a.dtype), a], axis=1)
    c = conv_b + sum(conv_w[j] * a_ext[:, j:j + T] for j in range(CONV_W))
    y = jax.nn.gelu(c) * b
    return jnp.einsum('btf,fd->btd', y, w_down), a_ext[:, T:]


def trunk_layer(x, p, lb, dsa_fn, hgrn_fn, mem_k, mem_v, conv_buf):
    B, T, _ = x.shape
    f32 = jnp.float32
    h = rms_norm(x, p['g_pre_mix'])
    z = jnp.einsum('btd,de->bte', h, p['w_in'])
    splits = np.cumsum(MIX_SIZES)[:-1].tolist()
    q, k, v, iq, iw, ik, hq, hf, hi, hg, ga, gb = jnp.split(z, splits, axis=-1)
    q = q.reshape(B, T, N_HEADS, HEAD_DIM)
    k = k.reshape(B, T, KV_HEADS, HEAD_DIM)
    v = v.reshape(B, T, KV_HEADS, HEAD_DIM)
    iq = iq.reshape(B, T, IDX_HEADS, IDX_DIM)
    iw = iw * (IDX_HEADS ** -0.5)
    o_a = dsa_fn(q, k, v, iq, iw, ik)
    f = lb + (1.0 - lb) * jax.nn.sigmoid(hf.astype(f32))
    kshape = (B, T, HG_HEADS, HG_DK)
    o_b, S_new = hgrn_fn(hq.astype(f32).reshape(kshape), (1.0 - f).reshape(kshape),
                         hi.astype(f32).reshape(B, T, HG_HEADS, HG_DV), jnp.log(f).reshape(kshape))
    o_b = rms_norm(o_b, p['hg_norm'].reshape(HG_HEADS, HG_DV)).reshape(B, T, B_WIDTH).astype(x.dtype) * jax.nn.silu(hg)
    u = jax.nn.sigmoid(ga) * o_a + jax.nn.sigmoid(gb) * o_b
    x = x + rms_norm(jnp.einsum('bte,ed->btd', u, p['w_out']), p['g_post_mix'])
    h = rms_norm(x, p['g_pre_x'])
    x = x + rms_norm(cross_attend(h, p['w_xq'], mem_k, mem_v, p['w_xo']), p['g_post_x'])
    h = rms_norm(x, p['g_pre_ffn'])
    y, new_buf = conv_ffn(h, conv_buf, p['w_up'], p['conv_w'], p['conv_b'], p['w_down'])
    x = x + rms_norm(y, p['g_post_ffn'])
    return x, k, v, ik, S_new.astype(x.dtype), new_buf


def setup_inputs(seed: int = 0) -> dict:
    key = jax.random.key(seed)
    keys = iter(jax.random.split(key, 48))

    def nrm(shape, scale):
        return jax.random.normal(next(keys), shape, jnp.float32) * scale

    def gain(shape):
        return 1.0 + nrm(shape, 0.05)

    n_pages = PAST_LEN // PAGE_SIZE
    n_used = DEC_BATCH * n_pages
    n_pool = n_used + n_used // 4
    page_table = jax.random.permutation(next(keys), n_pool)[:n_used].reshape(DEC_BATCH, n_pages).astype(jnp.int32)
    return {
        'x_prompt': nrm((BATCH, SEQ, D_MODEL), 1.0),
        'x_sample': nrm((DEC_BATCH, DEC_SEQ, D_MODEL), 1.0),
        'mem_prompt': nrm((BATCH, MEM_LEN, D_MODEL), 1.0),
        'cache_k': nrm((DEPTH, n_pool, PAGE_SIZE, KV_HEADS, HEAD_DIM), 1.0),
        'cache_v': nrm((DEPTH, n_pool, PAGE_SIZE, KV_HEADS, HEAD_DIM), 1.0),
        'cache_idx_k': nrm((DEPTH, n_pool, PAGE_SIZE, IDX_DIM), 1.0),
        'cache_mem_k': nrm((DEPTH, DEC_BATCH, MEM_LEN, X_HEADS, X_HEAD_DIM), 1.0),
        'cache_mem_v': nrm((DEPTH, DEC_BATCH, MEM_LEN, X_HEADS, X_HEAD_DIM), 1.0),
        'state_hgrn': nrm((DEPTH, DEC_BATCH, HG_HEADS, HG_DK, HG_DV), 0.5),
        'state_conv': nrm((DEPTH, DEC_BATCH, CONV_W - 1, D_FF), 1.0),
        'page_table': page_table,
        'rel_bias': nrm((N_BUCKETS, N_HEADS), 0.2),
        'hg_lb': nrm((DEPTH + 1, HG_KWIDTH), 0.1),
        'g_pre_mix': gain((DEPTH, D_MODEL)),
        'w_in': nrm((DEPTH, D_MODEL, MIX_WIDTH), D_MODEL ** -0.5),
        'hg_norm': gain((DEPTH, B_WIDTH)),
        'w_out': nrm((DEPTH, MERGE_WIDTH, D_MODEL), MERGE_WIDTH ** -0.5),
        'g_post_mix': gain((DEPTH, D_MODEL)),
        'g_pre_x': gain((DEPTH, D_MODEL)),
        'g_mem': gain((DEPTH, D_MODEL)),
        'w_xq': nrm((DEPTH, D_MODEL, D_MODEL), D_MODEL ** -0.5),
        'w_xk': nrm((DEPTH, D_MODEL, D_MODEL), D_MODEL ** -0.5),
        'w_xv': nrm((DEPTH, D_MODEL, D_MODEL), D_MODEL ** -0.5),
        'w_xo': nrm((DEPTH, D_MODEL, D_MODEL), D_MODEL ** -0.5),
        'g_post_x': gain((DEPTH, D_MODEL)),
        'g_pre_ffn': gain((DEPTH, D_MODEL)),
        'w_up': nrm((DEPTH, D_MODEL, 2 * D_FF), D_MODEL ** -0.5),
        'conv_w': nrm((DEPTH, CONV_W, D_FF), CONV_W ** -0.5),
        'conv_b': nrm((DEPTH, D_FF), 0.02),
        'w_down': nrm((DEPTH, D_FF, D_MODEL), D_FF ** -0.5),
        'g_post_ffn': gain((DEPTH, D_MODEL)),
    }


def reference(x_prompt, x_sample, mem_prompt, cache_k, cache_v, cache_idx_k, cache_mem_k, cache_mem_v,
              state_hgrn, state_conv, page_table, rel_bias, hg_lb, g_pre_mix, w_in, hg_norm, w_out,
              g_post_mix, g_pre_x, g_mem, w_xq, w_xk, w_xv, w_xo, g_post_x, g_pre_ffn, w_up, conv_w,
              conv_b, w_down, g_post_ffn):
    lbs = hgrn_lower_bounds(hg_lb)
    xp, xs = x_prompt, x_sample
    new = [[] for _ in range(12)]
    for l in range(DEPTH):
        p = {'g_pre_mix': g_pre_mix[l], 'w_in': w_in[l], 'hg_norm': hg_norm[l], 'w_out': w_out[l],
             'g_post_mix': g_post_mix[l], 'g_pre_x': g_pre_x[l], 'w_xq': w_xq[l], 'w_xo': w_xo[l],
             'g_post_x': g_post_x[l], 'g_pre_ffn': g_pre_ffn[l], 'w_up': w_up[l], 'conv_w': conv_w[l],
             'conv_b': conv_b[l], 'w_down': w_down[l], 'g_post_ffn': g_post_ffn[l]}
        mk, mv = mem_kv(mem_prompt, g_mem[l], w_xk[l], w_xv[l])
        conv0 = jnp.zeros((xp.shape[0], CONV_W - 1, D_FF), xp.dtype)
        xp, kp, vp, ikp, sp, cp = trunk_layer(
            xp, p, lbs[l], functools.partial(dsa_prompt, rel_bias=rel_bias), hgrn_prompt, mk, mv, conv0)
        dsa_s = functools.partial(dsa_sample, cache_k=cache_k[l], cache_v=cache_v[l],
                                  cache_idx_k=cache_idx_k[l], page_table=page_table, rel_bias=rel_bias)
        xs, ks, vs, iks, ss, cs = trunk_layer(
            xs, p, lbs[l], dsa_s, functools.partial(hgrn_sample, S0=state_hgrn[l]),
            cache_mem_k[l], cache_mem_v[l], state_conv[l])
        for store, val in zip(new, (kp, vp, ikp, ks, vs, iks, mk, mv, sp, ss, cp, cs)):
            store.append(val)
    (new_k_prompt, new_v_prompt, new_idx_k_prompt, new_k_sample, new_v_sample, new_idx_k_sample,
     new_mem_k_prompt, new_mem_v_prompt, new_hgrn_prompt, new_hgrn_sample,
     new_conv_prompt, new_conv_sample) = [jnp.stack(s, axis=0) for s in new]
    y_prompt, y_sample = xp, xs
    return (y_prompt, y_sample, new_k_prompt, new_v_prompt, new_idx_k_prompt, new_k_sample, new_v_sample,
            new_idx_k_sample, new_mem_k_prompt, new_mem_v_prompt, new_hgrn_prompt, new_hgrn_sample,
            new_conv_prompt, new_conv_sample)
```

```python
import functools
import math

import numpy as np
import jax
import jax.numpy as jnp
from jax import lax
from jax.experimental import pallas as pl
from jax.experimental.pallas import tpu as pltpu

F32 = jnp.float32
BF16 = jnp.bfloat16

EPS = 1e-6
TOPK_MAX = 256
MAX_DISTANCE = 128

V7X_VMEM_BYTES = 64 * 1024 * 1024
VMEM_LIMIT_BYTES = V7X_VMEM_BYTES - 8 * 1024 * 1024
LANE = 128


def _params(n_axes):
    return pltpu.CompilerParams(dimension_semantics=("arbitrary",) * n_axes, vmem_limit_bytes=VMEM_LIMIT_BYTES)


def _tile(n, cap, unit=LANE):
    if n <= cap:
        return n
    best = None
    for t in range(unit, cap + 1, unit):
        if n % t == 0:
            best = t
    assert best is not None, (n, cap, unit)
    return best


def _norm_matmul_body(x_ref, g_ref, w_ref, o_ref, h_ref):
    @pl.when(pl.program_id(1) == 0)
    def _():
        x = x_ref[...]
        ms = jnp.mean(x * x, axis=-1, keepdims=True)
        h_ref[...] = (x * lax.rsqrt(ms + EPS) * g_ref[...]).astype(BF16)

    o_ref[...] = jnp.dot(h_ref[...], w_ref[...], preferred_element_type=F32).astype(o_ref.dtype)


def norm_matmul(x, g, w, out_dtype, tm_cap=512, tn_cap=1024):
    m, k = x.shape
    n = w.shape[1]
    tm, tn = _tile(m, tm_cap, 8), _tile(n, tn_cap)
    return pl.pallas_call(
        _norm_matmul_body,
        grid=(m // tm, n // tn),
        in_specs=[pl.BlockSpec((tm, k), lambda i, j: (i, 0)),
                  pl.BlockSpec((1, k), lambda i, j: (0, 0)),
                  pl.BlockSpec((k, tn), lambda i, j: (0, j))],
        out_specs=pl.BlockSpec((tm, tn), lambda i, j: (i, j)),
        out_shape=jax.ShapeDtypeStruct((m, n), out_dtype),
        scratch_shapes=[pltpu.VMEM((tm, k), BF16)],
        compiler_params=_params(2),
        name="norm_matmul",
    )(x, g.reshape(1, k).astype(F32), w)


def _matmul_norm_res_body(a_ref, w_ref, g_ref, r_ref, o_ref, acc_ref):
    kk = pl.program_id(1)

    @pl.when(kk == 0)
    def _():
        acc_ref[...] = jnp.zeros_like(acc_ref)

    acc_ref[...] += jnp.dot(a_ref[...].astype(BF16), w_ref[...], preferred_element_type=F32)

    @pl.when(kk == pl.num_programs(1) - 1)
    def _():
        y = acc_ref[...]
        ms = jnp.mean(y * y, axis=-1, keepdims=True)
        o_ref[...] = r_ref[...] + y * lax.rsqrt(ms + EPS) * g_ref[...]


def matmul_norm_res(a, w, g, res, tm_cap=512, tk_cap=2048):
    m, k = a.shape
    n = w.shape[1]
    tm, tk = _tile(m, tm_cap, 8), _tile(k, tk_cap)
    return pl.pallas_call(
        _matmul_norm_res_body,
        grid=(m // tm, k // tk),
        in_specs=[pl.BlockSpec((tm, tk), lambda i, j: (i, j)),
                  pl.BlockSpec((tk, n), lambda i, j: (j, 0)),
                  pl.BlockSpec((1, n), lambda i, j: (0, 0)),
                  pl.BlockSpec((tm, n), lambda i, j: (i, 0))],
        out_specs=pl.BlockSpec((tm, n), lambda i, j: (i, 0)),
        out_shape=jax.ShapeDtypeStruct((m, n), F32),
        scratch_shapes=[pltpu.VMEM((tm, n), F32)],
        compiler_params=_params(2),
        name="matmul_norm_res",
    )(a, w, g.reshape(1, n).astype(F32), res)


def _mix_in_body(seg_tiles, x_ref, g_ref, w_ref, ws_ref, *rest):
    n_seg = len(seg_tiles)
    out_refs, side_ref, h_ref = rest[:n_seg], rest[n_seg], rest[n_seg + 1]
    j = pl.program_id(1)

    @pl.when(j == 0)
    def _():
        x = x_ref[...]
        ms = jnp.mean(x * x, axis=-1, keepdims=True)
        h = (x * lax.rsqrt(ms + EPS) * g_ref[...]).astype(BF16)
        h_ref[...] = h
        side_ref[...] = jnp.dot(h, ws_ref[...], preferred_element_type=F32)

    start = 0
    for o_ref, n_t in zip(out_refs, seg_tiles):
        @pl.when((j >= start) & (j < start + n_t))
        def _(o_ref=o_ref):
            o_ref[...] = jnp.dot(h_ref[...], w_ref[...], preferred_element_type=F32).astype(o_ref.dtype)
        start += n_t


def mix_in_proj(x, g, w_main, w_side, seg_widths, seg_dtypes, tn, tm_cap=512):
    m, k = x.shape
    tm = _tile(m, tm_cap, 8)
    seg_tiles = tuple(wd // tn for wd in seg_widths)
    starts = tuple(int(s) for s in np.cumsum((0,) + seg_tiles[:-1]))

    def seg_map(start, n_t):
        return lambda i, j: (i, jnp.clip(j - start, 0, n_t - 1))

    out_specs = [pl.BlockSpec((tm, tn), seg_map(s, n)) for s, n in zip(starts, seg_tiles)]
    out_specs.append(pl.BlockSpec((tm, LANE), lambda i, j: (i, 0)))
    out_shape = [jax.ShapeDtypeStruct((m, wd), dt) for wd, dt in zip(seg_widths, seg_dtypes)]
    out_shape.append(jax.ShapeDtypeStruct((m, LANE), F32))
    return pl.pallas_call(
        functools.partial(_mix_in_body, seg_tiles),
        grid=(m // tm, sum(seg_tiles)),
        in_specs=[pl.BlockSpec((tm, k), lambda i, j: (i, 0)),
                  pl.BlockSpec((1, k), lambda i, j: (0, 0)),
                  pl.BlockSpec((k, tn), lambda i, j: (0, j)),
                  pl.BlockSpec((k, LANE), lambda i, j: (0, 0))],
        out_specs=out_specs,
        out_shape=out_shape,
        scratch_shapes=[pltpu.VMEM((tm, k), BF16)],
        compiler_params=_params(2),
        name="mix_in_proj",
    )(x, g.reshape(1, k).astype(F32), w_main, w_side)


INT_MIN = -2 ** 31
MASK_NEG = -1e30
_NT = (((1,), (1,)), ((), ()))


def _sortable_key(x):
    i = pltpu.bitcast(x, jnp.int32)
    return i ^ (lax.shift_right_arithmetic(i, 31) & jnp.int32(0x7FFFFFFF))


def _bucket_thresholds(n_buckets, max_distance, d_max):
    max_exact = n_buckets // 2
    d = np.arange(d_max, dtype=np.int64)
    log_ratio = np.log(np.maximum(d, 1).astype(np.float32) / np.float32(max_exact)) / np.float32(math.log(max_distance / max_exact))
    large = np.minimum(max_exact + (log_ratio * np.float32(n_buckets - max_exact)).astype(np.int32), n_buckets - 1)
    bucket = np.where(d < max_exact, d, large)
    assert np.all(np.diff(bucket) >= 0)
    ths = [int(np.argmax(bucket >= max_exact + n)) for n in range(1, n_buckets - max_exact)]
    return max_exact, ths, bucket


def _kth_largest_threshold(count_ge, kf, shape):
    lo = jnp.where(count_ge(jnp.zeros(shape, jnp.int32)) >= kf, jnp.int32(0), jnp.int32(INT_MIN))

    def step(bi, lo):
        cand = lo + lax.shift_left(jnp.int32(1), 30 - bi)
        return jnp.where(count_ge(cand) >= kf, cand, lo)

    return lax.fori_loop(0, 31, step, lo)


def _dsa_prompt_body(cfg, relb_ref, q_ref, kv_ref, iq_ref, sq_ref, sk_ref, o_ref,
                     ikl_scr, ikh_scr, keys_scr, madd_scr, s_scr, m_scr, l_scr, acc_scr, bias_scr):
    tq, hd, kvh, qpk, ih, idd, topk, n_buckets = (cfg[k] for k in ("tq", "hd", "kvh", "qpk", "ih", "idd", "topk", "n_buckets"))
    kvw = kvh * hd
    nl = tq // LANE
    rows = qpk * tq
    scale = hd ** -0.5
    b, i = pl.program_id(0), pl.program_id(1)

    @pl.when((b == 0) & (i == 0))
    def _():
        max_exact, ths, _ = _bucket_thresholds(n_buckets, MAX_DISTANCE, 2 * tq)
        tl = lax.broadcasted_iota(jnp.int32, (tq, tq), 0)
        sl = lax.broadcasted_iota(jnp.int32, (tq, tq), 1)
        for r in range(2):
            d = jnp.maximum(r * tq + tl - sl, 0)
            bucket = jnp.minimum(d, max_exact)
            for th in ths:
                bucket = bucket + jnp.where(d >= th, 1, 0)
            for h in range(kvh * qpk):
                far = relb_ref[n_buckets - 1, h]
                val = jnp.full((tq, tq), far, F32)
                for bk in range(n_buckets - 1):
                    val = jnp.where(bucket == bk, relb_ref[bk, h], val)
                g, hl = divmod(h, qpk)
                bias_scr[g, r, hl * tq:(hl + 1) * tq, :] = (val - far) * (1.0 / scale)

    @pl.when(i == 0)
    def _():
        side = sk_ref[...]
        lane = lax.broadcasted_iota(jnp.int32, side.shape, 1)
        lo = jnp.where(lane < idd, side, 0.0)
        ikl_scr[...] = lo.astype(BF16)
        ikh_scr[...] = pltpu.roll(lo, idd, axis=1).astype(BF16)

    w_rows = sq_ref[...].T[idd:idd + ih, :] * (ih ** -0.5 * idd ** -0.5)
    t_pos = i * tq + lax.broadcasted_iota(jnp.int32, (tq, tq), 1)
    s_loc = lax.broadcasted_iota(jnp.int32, (tq, tq), 0)

    def index_chunk(c, carry):
        r0 = pl.multiple_of(c * tq, tq)
        kl = ikl_scr[pl.ds(r0, tq), :]
        kh = ikh_scr[pl.ds(r0, tq), :]
        acc = jnp.zeros((tq, tq), F32)
        for jp in range(ih // 2):
            iqp = iq_ref[:, jp * LANE:(jp + 1) * LANE]
            x0 = lax.dot_general(kl, iqp, _NT, preferred_element_type=F32)
            x1 = lax.dot_general(kh, iqp, _NT, preferred_element_type=F32)
            acc = acc + jnp.maximum(x0, 0.0) * w_rows[2 * jp:2 * jp + 1, :] + jnp.maximum(x1, 0.0) * w_rows[2 * jp + 1:2 * jp + 2, :]
        keys_scr[c] = jnp.where(c * tq + s_loc <= t_pos, _sortable_key(acc), jnp.int32(INT_MIN))
        return carry

    lax.fori_loop(0, i + 1, index_chunk, 0)

    def count_ge(cand):
        def chunk(c, acc):
            hit = jnp.where(keys_scr[c] >= cand, 1.0, 0.0)
            return acc + jnp.sum(hit.reshape(tq // 8, 8, tq), axis=0)
        acc = lax.fori_loop(0, i + 1, chunk, jnp.zeros((8, tq), F32))
        return jnp.sum(acc, axis=0, keepdims=True)

    thr = _kth_largest_threshold(count_ge, float(topk), (1, tq))
    thr = jnp.maximum(thr, jnp.int32(INT_MIN + 1))

    n_ge = count_ge(thr)
    has_ties = jnp.max(n_ge) > float(topk)

    @pl.when(jnp.logical_not(has_ties))
    def _():
        def mask_chunk(c, carry):
            madd_scr[c] = jnp.where(keys_scr[c] >= thr, 0.0, MASK_NEG).T
            return carry

        lax.fori_loop(0, i + 1, mask_chunk, 0)

    @pl.when(has_ties)
    def _():
        need = float(topk) - count_ge(thr + 1)
        lower_tri = jnp.where(lax.broadcasted_iota(jnp.int32, (tq, tq), 0) >= lax.broadcasted_iota(jnp.int32, (tq, tq), 1),
                              1.0, 0.0).astype(BF16)

        def mask_chunk(c, seen):
            kc = keys_scr[c]
            tie = jnp.where(kc == thr, 1.0, 0.0)
            rank = seen + jnp.dot(lower_tri, tie.astype(BF16), preferred_element_type=F32)
            take = (kc > thr) | ((kc == thr) & (rank <= need))
            madd_scr[c] = jnp.where(take, 0.0, MASK_NEG).T
            return seen + jnp.sum(tie, axis=0, keepdims=True)

        lax.fori_loop(0, i + 1, mask_chunk, jnp.zeros((1, tq), F32))

    cexp = scale * math.log2(math.e)

    def fold(x, op):
        y = x[:, :LANE]
        for t in range(1, nl):
            y = op(y, x[:, t * LANE:(t + 1) * LANE])
        return y

    for g in range(kvh):
        qg = jnp.concatenate([q_ref[:, (g * qpk + hl) * hd:(g * qpk + hl + 1) * hd] for hl in range(qpk)], axis=0)
        m_scr[...] = jnp.full((rows, LANE), -jnp.inf, F32)

        def pass1(c, bias_r, qg=qg, g=g):
            r0 = pl.multiple_of(c * tq, tq)
            kc = kv_ref[pl.ds(r0, tq), g * hd:(g + 1) * hd]
            z = lax.dot_general(qg, kc, _NT, preferred_element_type=F32)
            md = madd_scr[c]
            z = z + jnp.concatenate([md] * qpk, axis=0)
            if bias_r is not None:
                z = z + bias_scr[g, bias_r]
            s_scr[c] = z
            m_scr[...] = jnp.maximum(m_scr[...], fold(z, jnp.maximum))

        def far_chunk(c, carry):
            pass1(c, None)
            return carry

        lax.fori_loop(0, jnp.maximum(i - 1, 0), far_chunk, 0)

        @pl.when(i >= 1)
        def _():
            pass1(i - 1, 1)

        pass1(i, 0)

        m_row = jnp.max(m_scr[...], axis=1, keepdims=True)
        l_scr[...] = jnp.zeros((rows, LANE), F32)
        acc_scr[...] = jnp.zeros((rows, hd), F32)

        def pass2(c, carry, g=g, m_row=m_row):
            r0 = pl.multiple_of(c * tq, tq)
            p = jnp.exp2((s_scr[c] - m_row) * cexp)
            l_scr[...] += fold(p, jnp.add)
            vc = kv_ref[pl.ds(r0, tq), kvw + g * hd:kvw + (g + 1) * hd]
            acc_scr[...] += jnp.dot(p.astype(BF16), vc, preferred_element_type=F32)
            return carry

        lax.fori_loop(0, i + 1, pass2, 0)
        l_row = jnp.sum(l_scr[...], axis=1, keepdims=True)
        o = acc_scr[...] / l_row
        for hl in range(qpk):
            h = g * qpk + hl
            o_ref[:, h * hd:(h + 1) * hd] = o[hl * tq:(hl + 1) * tq, :].astype(o_ref.dtype)


def dsa_prompt(q, kv, iq, side, rel_bias, batch, seq, cfg, tq_cap=256):
    m, a_width = q.shape
    hd, kvh, ih, idd = cfg["hd"], cfg["kvh"], cfg["ih"], cfg["idd"]
    n_heads = a_width // hd
    qpk = n_heads // kvh
    tq = _tile(seq, tq_cap)
    n_buckets = rel_bias.shape[0]
    assert 2 * idd == LANE and ih % 2 == 0 and idd + ih <= LANE and seq % tq == 0
    _, _, bucket = _bucket_thresholds(n_buckets, MAX_DISTANCE, 2 * tq)
    assert bucket[tq + 1] == n_buckets - 1, "bias must be saturated two key chunks away from the diagonal"
    nq = seq // tq
    kcfg = dict(tq=tq, hd=hd, kvh=kvh, qpk=qpk, ih=ih, idd=idd, topk=min(TOPK_MAX, seq // 4), n_buckets=n_buckets)
    rows = qpk * tq
    return pl.pallas_call(
        functools.partial(_dsa_prompt_body, kcfg),
        grid=(batch, nq),
        in_specs=[pl.BlockSpec(memory_space=pltpu.SMEM),
                  pl.BlockSpec((tq, a_width), lambda b, i: (b * nq + i, 0)),
                  pl.BlockSpec((seq, kv.shape[1]), lambda b, i: (b, 0)),
                  pl.BlockSpec((tq, iq.shape[1]), lambda b, i: (b * nq + i, 0)),
                  pl.BlockSpec((tq, LANE), lambda b, i: (b * nq + i, 0)),
                  pl.BlockSpec((seq, LANE), lambda b, i: (b, 0))],
        out_specs=pl.BlockSpec((tq, a_width), lambda b, i: (b * nq + i, 0)),
        out_shape=jax.ShapeDtypeStruct((m, a_width), F32),
        scratch_shapes=[pltpu.VMEM((seq, LANE), BF16), pltpu.VMEM((seq, LANE), BF16),
                        pltpu.VMEM((nq, tq, tq), jnp.int32), pltpu.VMEM((nq, tq, tq), F32),
                        pltpu.VMEM((nq, rows, tq), F32), pltpu.VMEM((rows, LANE), F32),
                        pltpu.VMEM((rows, LANE), F32), pltpu.VMEM((rows, hd), F32),
                        pltpu.VMEM((kvh, 2, rows, tq), F32)],
        compiler_params=_params(2),
        name="dsa_prompt",
    )(rel_bias.astype(F32), q, kv, iq, side, side)


def _paged_specs(n_pages, block, table_arg):
    return [pl.BlockSpec((1,) + block, lambda b, pt, p=p: (pt[b, p],) + (0,) * len(block)) for p in range(n_pages)]


def _dsa_sample_scores_body(cfg, pt_ref, iq_ref, iw_ref, *rest):
    n_pages, ih, idd, t_seq = (cfg[k] for k in ("n_pages", "ih", "idd", "t_seq"))
    page_refs, new_ref, keys_ref = rest[:n_pages], rest[n_pages], rest[n_pages + 1]
    iq = iq_ref[0]
    w = iw_ref[0] * (ih ** -0.5 * idd ** -0.5)
    page = new_ref.shape[1]
    t = lax.broadcasted_iota(jnp.int32, (t_seq, page), 0)
    j = lax.broadcasted_iota(jnp.int32, (t_seq, page), 1)
    for c in range(n_pages + 1):
        ikc = (page_refs[c] if c < n_pages else new_ref)[0].astype(BF16)
        x = lax.dot_general(iq, ikc, _NT, preferred_element_type=F32)
        sc = jnp.sum((jnp.maximum(x, 0.0) * w).reshape(ih, t_seq, page), axis=0)
        key = _sortable_key(sc)
        if c == n_pages:
            key = jnp.where(j <= t, key, jnp.int32(INT_MIN))
        keys_ref[0, c] = key


def _dsa_sample_mask_body(topk, keys_ref, madd_ref):
    nb, n_chunks, t_seq, page = keys_ref.shape
    rows = nb * t_seq

    def chunk(c):
        return keys_ref[:, c].reshape(rows, page)

    def count_ge(cand):
        acc = jnp.zeros((rows, page), F32)
        for c in range(n_chunks):
            acc = acc + jnp.where(chunk(c) >= cand, 1.0, 0.0)
        return jnp.sum(acc, axis=1, keepdims=True)

    thr = _kth_largest_threshold(count_ge, float(topk), (rows, 1))
    thr = jnp.maximum(thr, jnp.int32(INT_MIN + 1))
    has_ties = jnp.max(count_ge(thr)) > float(topk)

    @pl.when(jnp.logical_not(has_ties))
    def _():
        for c in range(n_chunks):
            madd_ref[:, c] = jnp.where(chunk(c) >= thr, 0.0, MASK_NEG).reshape(nb, t_seq, page)

    @pl.when(has_ties)
    def _():
        need = float(topk) - count_ge(thr + 1)
        upper_tri = jnp.where(lax.broadcasted_iota(jnp.int32, (page, page), 0) <= lax.broadcasted_iota(jnp.int32, (page, page), 1),
                              1.0, 0.0).astype(BF16)
        seen = jnp.zeros((rows, 1), F32)
        for c in range(n_chunks):
            kc = chunk(c)
            tie = jnp.where(kc == thr, 1.0, 0.0)
            rank = seen + jnp.dot(tie.astype(BF16), upper_tri, preferred_element_type=F32)
            take = (kc > thr) | ((kc == thr) & (rank <= need))
            madd_ref[:, c] = jnp.where(take, 0.0, MASK_NEG).reshape(nb, t_seq, page)
            seen = seen + jnp.sum(tie, axis=1, keepdims=True)


def _dsa_sample_attend_body(cfg, pt_ref, relb_ref, q_ref, madd_ref, *rest):
    n_pages, hd, kvh, qpk, t_seq, n_buckets = (cfg[k] for k in ("n_pages", "hd", "kvh", "qpk", "t_seq", "n_buckets"))
    k_refs, v_refs = rest[:n_pages], rest[n_pages:2 * n_pages]
    new_ref, o_ref, s_scr, bias_scr = rest[2 * n_pages:]
    n_chunks = n_pages + 1
    n_heads = kvh * qpk
    kvw = kvh * hd
    rows = n_heads * t_seq
    grows = qpk * t_seq
    page = new_ref.shape[1]
    scale = hd ** -0.5

    @pl.when(pl.program_id(0) == 0)
    def _():
        max_exact, ths, _ = _bucket_thresholds(n_buckets, MAX_DISTANCE, 2 * page)
        t = lax.broadcasted_iota(jnp.int32, (t_seq, page), 0)
        j = lax.broadcasted_iota(jnp.int32, (t_seq, page), 1)
        for r in range(2):
            d = jnp.maximum(r * page + t - j, 0)
            bucket = jnp.minimum(d, max_exact)
            for th in ths:
                bucket = bucket + jnp.where(d >= th, 1, 0)
            for h in range(n_heads):
                far = relb_ref[n_buckets - 1, h]
                val = jnp.full((t_seq, page), far, F32)
                for bk in range(n_buckets - 1):
                    val = jnp.where(bucket == bk, relb_ref[bk, h], val)
                bias_scr[r, h * t_seq:(h + 1) * t_seq, :] = (val - far) * (1.0 / scale)

    q = q_ref[0]
    m = jnp.full((rows, page), -jnp.inf, F32)
    for c in range(n_chunks):
        kc = (k_refs[c][0] if c < n_pages else new_ref[0, :, :kvw]).astype(BF16)
        z = jnp.concatenate([lax.dot_general(q[g * grows:(g + 1) * grows, :], kc[:, g * hd:(g + 1) * hd], _NT,
                                             preferred_element_type=F32) for g in range(kvh)], axis=0)
        z = z + jnp.concatenate([madd_ref[0, c]] * n_heads, axis=0)
        if c >= n_chunks - 2:
            z = z + bias_scr[n_chunks - 1 - c]
        s_scr[c] = z
        m = jnp.maximum(m, z)
    m_row = jnp.max(m, axis=1, keepdims=True)
    cexp = scale * math.log2(math.e)
    l_acc = jnp.zeros((rows, page), F32)
    acc = [jnp.zeros((grows, hd), F32) for _ in range(kvh)]
    for c in range(n_chunks):
        p = jnp.exp2((s_scr[c] - m_row) * cexp)
        l_acc = l_acc + p
        vc = (v_refs[c][0] if c < n_pages else new_ref[0, :, kvw:]).astype(BF16)
        pb = p.astype(BF16)
        for g in range(kvh):
            acc[g] = acc[g] + jnp.dot(pb[g * grows:(g + 1) * grows, :], vc[:, g * hd:(g + 1) * hd], preferred_element_type=F32)
    l_row = jnp.sum(l_acc, axis=1, keepdims=True)
    o_ref[0] = jnp.concatenate(acc, axis=0) / l_row


def dsa_sample(q, kv, iq, side, cache_k, cache_v, cache_idx_k, page_table, rel_bias, cfg, seqs_per_mask_step=16):
    hd, kvh, ih, idd = cfg["hd"], cfg["kvh"], cfg["ih"], cfg["idd"]
    n_seq, n_pages = page_table.shape
    n_pool, page = cache_k.shape[:2]
    m, a_width = q.shape
    t_seq = m // n_seq
    n_heads = a_width // hd
    qpk = n_heads // kvh
    kvw = kvh * hd
    n_chunks = n_pages + 1
    n_buckets = rel_bias.shape[0]
    assert t_seq == SUBLANES and page == LANE
    _, _, bucket = _bucket_thresholds(n_buckets, MAX_DISTANCE, 2 * page)
    assert bucket[page + 1] == n_buckets - 1, "bias must be saturated two pages before the new tokens"
    topk = min(TOPK_MAX, (n_pages * page + t_seq) // 4)
    iq_ht = iq.reshape(n_seq, t_seq, ih, idd).transpose(0, 2, 1, 3).reshape(n_seq, ih * t_seq, idd)
    iw_ht = side[:, idd:idd + ih].reshape(n_seq, t_seq, ih).transpose(0, 2, 1).reshape(n_seq, ih * t_seq, 1)
    q_ht = q.reshape(n_seq, t_seq, n_heads, hd).transpose(0, 2, 1, 3).reshape(n_seq, n_heads * t_seq, hd)
    pad_rows = ((0, 0), (0, page - t_seq), (0, 0))
    ik_new = jnp.pad(side[:, :idd].reshape(n_seq, t_seq, idd), pad_rows)
    kv_new = jnp.pad(kv.reshape(n_seq, t_seq, 2 * kvw), pad_rows)
    cache_k2 = cache_k.reshape(n_pool, page, kvw)
    cache_v2 = cache_v.reshape(n_pool, page, kvw)
    per_seq = lambda shape: pl.BlockSpec((1,) + shape, lambda b, pt: (b,) + (0,) * len(shape))

    scfg = dict(n_pages=n_pages, ih=ih, idd=idd, t_seq=t_seq)
    keys = pl.pallas_call(
        functools.partial(_dsa_sample_scores_body, scfg),
        grid_spec=pltpu.PrefetchScalarGridSpec(
            num_scalar_prefetch=1, grid=(n_seq,),
            in_specs=[per_seq((ih * t_seq, idd)), per_seq((ih * t_seq, 1))]
                     + _paged_specs(n_pages, (page, idd), 0) + [per_seq((page, idd))],
            out_specs=per_seq((n_chunks, t_seq, page))),
        out_shape=jax.ShapeDtypeStruct((n_seq, n_chunks, t_seq, page), jnp.int32),
        compiler_params=_params(1),
        name="dsa_sample_scores",
    )(page_table, iq_ht, iw_ht, *([cache_idx_k] * n_pages), ik_new)

    nb = _tile(n_seq, seqs_per_mask_step, 1)
    blk = pl.BlockSpec((nb, n_chunks, t_seq, page), lambda i: (i, 0, 0, 0))
    madd = pl.pallas_call(
        functools.partial(_dsa_sample_mask_body, topk),
        grid=(n_seq // nb,),
        in_specs=[blk],
        out_specs=blk,
        out_shape=jax.ShapeDtypeStruct((n_seq, n_chunks, t_seq, page), F32),
        compiler_params=_params(1),
        name="dsa_sample_mask",
    )(keys)

    acfg = dict(n_pages=n_pages, hd=hd, kvh=kvh, qpk=qpk, t_seq=t_seq, n_buckets=n_buckets)
    o_ht = pl.pallas_call(
        functools.partial(_dsa_sample_attend_body, acfg),
        grid_spec=pltpu.PrefetchScalarGridSpec(
            num_scalar_prefetch=1, grid=(n_seq,),
            in_specs=[pl.BlockSpec(memory_space=pltpu.SMEM), per_seq((n_heads * t_seq, hd)), per_seq((n_chunks, t_seq, page))]
                     + _paged_specs(n_pages, (page, kvw), 0) + _paged_specs(n_pages, (page, kvw), 0)
                     + [per_seq((page, 2 * kvw))],
            out_specs=per_seq((n_heads * t_seq, hd)),
            scratch_shapes=[pltpu.VMEM((n_chunks, n_heads * t_seq, page), F32),
                            pltpu.VMEM((2, n_heads * t_seq, page), F32)]),
        out_shape=jax.ShapeDtypeStruct((n_seq, n_heads * t_seq, hd), F32),
        compiler_params=_params(1),
        name="dsa_sample_attend",
    )(page_table, rel_bias.astype(F32), q_ht, madd, *([cache_k2] * n_pages), *([cache_v2] * n_pages), kv_new)
    return o_ht.reshape(n_seq, n_heads, t_seq, hd).transpose(0, 2, 1, 3).reshape(m, a_width)


SUBLANES = 8
GROUP_EXP_CLIP = 40.0
_TN = (((0,), (0,)), ((), ()))


def _split3_bf16(x):
    h1 = x.astype(BF16)
    r1 = x - h1.astype(F32)
    h2 = r1.astype(BF16)
    h3 = (r1 - h2.astype(F32)).astype(BF16)
    return jnp.concatenate([h1, h2, h3], axis=1)


def _block_row(b, m, row):
    c, w = b.shape
    b3 = b.reshape(c // m, m, w)
    return jnp.broadcast_to(b3[:, row:row + 1, :], (c // m, m, w)).reshape(c, w)


def _lower_bound(lb_logits, layer):
    e = jnp.exp(lb_logits - jnp.max(lb_logits, axis=0, keepdims=True))
    sm = e / jnp.sum(e, axis=0, keepdims=True)
    return jnp.sum(sm[1:layer + 2, :], axis=0, keepdims=True)


def _hgrn_gates(hf, lb):
    f = lb + (1.0 - lb) * jax.nn.sigmoid(hf)
    return jnp.log(f), 1.0 - f


def _hgrn_intra(q, kk, b, block, level_of_pair):
    c = q.shape[0]
    a = jnp.zeros((c, c), F32)
    m = block
    while m > SUBLANES:
        ref_row = _block_row(b, m, m // 2 - 1)
        qs = (q * jnp.exp(jnp.minimum(b - ref_row, 0.0))).astype(BF16)
        ks = (kk * jnp.exp(jnp.minimum(ref_row - b, 0.0))).astype(BF16)
        pr = lax.dot_general(qs, ks, _NT, preferred_element_type=F32)
        a = jnp.where(level_of_pair == m, pr, a)
        m //= 2
    ref_row = _block_row(b, SUBLANES, SUBLANES // 2 - 1)
    qs = (q * jnp.exp(jnp.minimum(b - ref_row, GROUP_EXP_CLIP))).astype(BF16)
    ks = (kk * jnp.exp(jnp.minimum(ref_row - b, GROUP_EXP_CLIP))).astype(BF16)
    pr = lax.dot_general(qs, ks, _NT, preferred_element_type=F32)
    return jnp.where(level_of_pair == SUBLANES, pr, a)


def _pair_levels(c, block):
    t = lax.broadcasted_iota(jnp.int32, (c, c), 0)
    s = lax.broadcasted_iota(jnp.int32, (c, c), 1)
    x = t ^ s
    lvl = jnp.full((c, c), SUBLANES, jnp.int32)
    m = SUBLANES
    while m < block:
        lvl = jnp.where(x >= m, 2 * m, lvl)
        m *= 2
    return jnp.where((s <= t) & (x < block), lvl, 0)


def _head_norm_gate(o, gn, hg):
    y = o * lax.rsqrt(jnp.mean(o * o, axis=-1, keepdims=True) + EPS) * gn
    return y * (hg * jax.nn.sigmoid(hg))


def _hgrn_prompt_body(chunk, layer, hq_ref, hf_ref, hi_ref, hg_ref, lb_ref, gn_ref, o_ref, s_ref, st_scr):
    seq, dk = hq_ref.shape
    c = chunk
    lb, gn = _lower_bound(lb_ref[...], layer), gn_ref[...]
    t = lax.broadcasted_iota(jnp.int32, (c, c), 0)
    s = lax.broadcasted_iota(jnp.int32, (c, c), 1)
    cum = jnp.where(s <= t, 1.0, 0.0).astype(BF16)
    levels = _pair_levels(c, c)
    st_scr[...] = jnp.zeros_like(st_scr)

    def step(ci, carry):
        r0 = pl.multiple_of(ci * c, c)
        rows = pl.ds(r0, c)
        q, v = hq_ref[rows, :], hi_ref[rows, :]
        logf, kk = _hgrn_gates(hf_ref[rows, :], lb)
        b3 = jnp.dot(cum, _split3_bf16(logf), preferred_element_type=F32)
        b = b3[:, :dk] + b3[:, dk:2 * dk] + b3[:, 2 * dk:]
        a = _hgrn_intra(q, kk, b, c, levels)
        st = st_scr[...]
        o = jnp.dot(a.astype(BF16), v.astype(BF16), preferred_element_type=F32)
        o = o + lax.dot_general((q * jnp.exp(b)).astype(BF16), st.astype(BF16), _NT, preferred_element_type=F32)
        b_end = b[c - 1:c, :]
        kd = (kk * jnp.exp(b_end - b)).astype(BF16)
        st_scr[...] = st * jnp.exp(b_end) + lax.dot_general(v.astype(BF16), kd, _TN, preferred_element_type=F32)
        o_ref[rows, :] = _head_norm_gate(o, gn, hg_ref[rows, :])
        return carry

    lax.fori_loop(0, seq // c, step, 0)
    s_ref[0, 0] = st_scr[...].T


def hgrn_prompt(hz, hg_lb, layer, hg_norm, batch, seq, heads, chunk=128):
    m, d4 = hz.shape
    d = d4 // 4
    dk = d // heads
    assert dk == LANE and seq % chunk == 0
    spec = lambda part: pl.BlockSpec((seq, dk), lambda b, h: (b, part * heads + h))
    vec = pl.BlockSpec((1, dk), lambda b, h: (0, h))
    lb_spec = pl.BlockSpec((hg_lb.shape[0], dk), lambda b, h: (0, h))
    return pl.pallas_call(
        functools.partial(_hgrn_prompt_body, chunk, layer),
        grid=(batch, heads),
        in_specs=[spec(0), spec(1), spec(2), spec(3), lb_spec, vec],
        out_specs=[pl.BlockSpec((seq, dk), lambda b, h: (b, h)),
                   pl.BlockSpec((1, 1, dk, dk), lambda b, h: (b, h, 0, 0))],
        out_shape=[jax.ShapeDtypeStruct((m, d), F32), jax.ShapeDtypeStruct((batch, heads, dk, dk), F32)],
        scratch_shapes=[pltpu.VMEM((dk, dk), F32)],
        compiler_params=_params(2),
        name="hgrn_prompt",
    )(hz, hz, hz, hz, hg_lb.astype(F32), hg_norm.reshape(1, d).astype(F32))


def _hgrn_sample_body(t_seq, layer, hq_ref, hf_ref, hi_ref, hg_ref, lb_ref, gn_ref, s0_ref, o_ref, s_ref):
    c, dk = hq_ref.shape
    nb = c // t_seq
    lb, gn = _lower_bound(lb_ref[...], layer), gn_ref[...]
    t = lax.broadcasted_iota(jnp.int32, (c, c), 0)
    s = lax.broadcasted_iota(jnp.int32, (c, c), 1)
    same_seq = (t ^ s) < t_seq
    cum = jnp.where((s <= t) & same_seq, 1.0, 0.0).astype(BF16)
    levels = _pair_levels(c, t_seq)
    q, v = hq_ref[...], hi_ref[...]
    logf, kk = _hgrn_gates(hf_ref[...], lb)
    b3 = jnp.dot(cum, _split3_bf16(logf), preferred_element_type=F32)
    b = b3[:, :dk] + b3[:, dk:2 * dk] + b3[:, 2 * dk:]
    a = _hgrn_intra(q, kk, b, t_seq, levels)
    o = jnp.dot(a.astype(BF16), v.astype(BF16), preferred_element_type=F32)
    qd = (q * jnp.exp(b)).astype(BF16)
    b_end_rows = _block_row(b, t_seq, t_seq - 1)
    kd = (kk * jnp.exp(b_end_rows - b)).astype(BF16)
    row = lax.broadcasted_iota(jnp.int32, (c, dk), 0)
    for n in range(nb):
        in_seq = (row >= n * t_seq) & (row < (n + 1) * t_seq)
        st = s0_ref[n, 0].T
        o = o + jnp.where(in_seq, lax.dot_general(qd, st.astype(BF16), _NT, preferred_element_type=F32), 0.0)
        b_end = b[(n + 1) * t_seq - 1:(n + 1) * t_seq, :]
        vn = jnp.where(in_seq, v, 0.0).astype(BF16)
        st_new = st * jnp.exp(b_end) + lax.dot_general(vn, kd, _TN, preferred_element_type=F32)
        s_ref[n, 0] = st_new.T
    o_ref[...] = _head_norm_gate(o, gn, hg_ref[...])


def hgrn_sample(hz, hg_lb, layer, hg_norm, state, t_seq, rows_per_step=128):
    m, d4 = hz.shape
    d = d4 // 4
    n_seq, heads, dk, dv = state.shape
    assert dk == LANE and dv == LANE and t_seq == SUBLANES and m == n_seq * t_seq
    c = _tile(m, rows_per_step, t_seq)
    nb = c // t_seq
    spec = lambda part: pl.BlockSpec((c, dk), lambda i, h: (i, part * heads + h))
    vec = pl.BlockSpec((1, dk), lambda i, h: (0, h))
    st_spec = pl.BlockSpec((nb, 1, dk, dv), lambda i, h: (i, h, 0, 0))
    lb_spec = pl.BlockSpec((hg_lb.shape[0], dk), lambda i, h: (0, h))
    return pl.pallas_call(
        functools.partial(_hgrn_sample_body, t_seq, layer),
        grid=(m // c, heads),
        in_specs=[spec(0), spec(1), spec(2), spec(3), lb_spec, vec, st_spec],
        out_specs=[pl.BlockSpec((c, dk), lambda i, h: (i, h)), st_spec],
        out_shape=[jax.ShapeDtypeStruct((m, d), F32), jax.ShapeDtypeStruct(state.shape, F32)],
        compiler_params=_params(2),
        name="hgrn_sample",
    )(hz, hz, hz, hz, hg_lb.astype(F32), hg_norm.reshape(1, d).astype(F32), state)


def _merge_out_body(oa_ref, ob_ref, ga_ref, gb_ref, w_ref, g_ref, r_ref, o_ref):
    u = jax.nn.sigmoid(ga_ref[...]) * oa_ref[...] + jax.nn.sigmoid(gb_ref[...]) * ob_ref[...]
    y = jnp.dot(u.astype(BF16), w_ref[...], preferred_element_type=F32)
    ms = jnp.mean(y * y, axis=-1, keepdims=True)
    o_ref[...] = r_ref[...] + y * lax.rsqrt(ms + EPS) * g_ref[...]


def merge_out_proj(o_a, o_b, gz, w, g, res, tm_cap=256):
    m, d = o_a.shape
    n = w.shape[1]
    tm = _tile(m, tm_cap, 8)
    row = lambda i: (i, 0)
    return pl.pallas_call(
        _merge_out_body,
        grid=(m // tm,),
        in_specs=[pl.BlockSpec((tm, d), row), pl.BlockSpec((tm, d), row),
                  pl.BlockSpec((tm, d), row), pl.BlockSpec((tm, d), lambda i: (i, 1)),
                  pl.BlockSpec((d, n), lambda i: (0, 0)), pl.BlockSpec((1, n), lambda i: (0, 0)),
                  pl.BlockSpec((tm, n), row)],
        out_specs=pl.BlockSpec((tm, n), row),
        out_shape=jax.ShapeDtypeStruct((m, n), F32),
        compiler_params=_params(1),
        name="merge_out_proj",
    )(o_a, o_b, gz, gz, w, g.reshape(1, n).astype(F32), res)


def _xattn_body(xh, q_ref, mk_ref, mv_ref, o_ref):
    d = q_ref.shape[1]
    xd = d // xh
    for h in range(xh):
        cols = slice(h * xd, (h + 1) * xd)
        qh = q_ref[:, cols].astype(BF16)
        s = lax.dot_general(qh, mk_ref[0, :, cols].astype(BF16), _NT, preferred_element_type=F32) * (xd ** -0.5)
        p = jnp.exp(s - jnp.max(s, axis=-1, keepdims=True))
        l_row = jnp.sum(p, axis=-1, keepdims=True)
        o = jnp.dot(p.astype(BF16), mv_ref[0, :, cols].astype(BF16), preferred_element_type=F32)
        o_ref[:, cols] = (o / l_row).astype(o_ref.dtype)


def cross_attend(q, mem_k, mem_v, batch, t_seq, xh, out_dtype, tq_cap=512):
    m, d = q.shape
    mlen = mem_k.shape[1]
    tq = _tile(t_seq, tq_cap, 8)
    nq = t_seq // tq
    mem_spec = pl.BlockSpec((1, mlen, d), lambda b, i: (b, 0, 0))
    return pl.pallas_call(
        functools.partial(_xattn_body, xh),
        grid=(batch, nq),
        in_specs=[pl.BlockSpec((tq, d), lambda b, i: (b * nq + i, 0)), mem_spec, mem_spec],
        out_specs=pl.BlockSpec((tq, d), lambda b, i: (b * nq + i, 0)),
        out_shape=jax.ShapeDtypeStruct((m, d), out_dtype),
        compiler_params=_params(2),
        name="cross_attend",
    )(q, mem_k, mem_v)


def _gelu_tanh(x):
    return 0.5 * x * (1.0 + jnp.tanh(math.sqrt(2.0 / math.pi) * (x + 0.044715 * x * x * x)))


def _ffn_up_body(tiles_per_seq, x_ref, g_ref, wa_ref, wb_ref, cw_ref, cb_ref, p1_ref, p2_ref, y_ref, tail_ref, h_ref, halo_ref):
    i, j = pl.program_id(0), pl.program_id(1)
    tm, tn = y_ref.shape

    @pl.when(j == 0)
    def _():
        x = x_ref[...]
        ms = jnp.mean(x * x, axis=-1, keepdims=True)
        h_ref[...] = (x * lax.rsqrt(ms + EPS) * g_ref[...]).astype(BF16)

    h = h_ref[...]
    a = jnp.dot(h, wa_ref[...], preferred_element_type=F32)
    b = jnp.dot(h, wb_ref[...], preferred_element_type=F32)
    row = lax.broadcasted_iota(jnp.int32, (tm, tn), 0)
    if tiles_per_seq:
        @pl.when(i % tiles_per_seq == 0)
        def _():
            halo_ref[j] = jnp.zeros((SUBLANES, tn), F32)

        halo = halo_ref[j]
        a1 =jnp.where(row == 0, halo[SUBLANES - 1:SUBLANES, :], pltpu.roll(a, 1, axis=0))
        a2 = jnp.where(row == 0, halo[SUBLANES - 2:SUBLANES - 1, :],
                       jnp.where(row == 1, halo[SUBLANES - 1:SUBLANES, :], pltpu.roll(a, 2, axis=0)))
        halo_ref[j] = a[tm - SUBLANES:, :]
    else:
        t_in_seq = row % SUBLANES
        a1 = jnp.where(t_in_seq == 0, p1_ref[...], pltpu.roll(a, 1, axis=0))
        a2 = jnp.where(t_in_seq <= 1, p2_ref[...], pltpu.roll(a, 2, axis=0))
    cw = cw_ref[...]
    c = cb_ref[...] + cw[0:1, :] * a2 + cw[1:2, :] * a1 + cw[2:3, :] * a
    y_ref[...] = (_gelu_tanh(c) * b).astype(y_ref.dtype)
    tail_ref[0] = a[tm - tail_ref.shape[1]:, :]


def ffn_up(x, g, w_up, conv_w, conv_b, d_ff, t_seq, prev1=None, prev2=None, tm_cap=512, tn_cap=1024):
    m, k = x.shape
    assert conv_w.shape[0] == 3
    tn = _tile(d_ff, tn_cap)
    nj = d_ff // tn
    if prev1 is None:
        tm = _tile(t_seq, tm_cap, SUBLANES)
        tiles_per_seq = t_seq // tm
        tail_rows = SUBLANES
        prev1 = prev2 = jnp.zeros((SUBLANES, LANE), F32)
        prev_spec = pl.BlockSpec((SUBLANES, LANE), lambda i, j: (0, 0))
    else:
        assert t_seq == SUBLANES
        tm = _tile(m, tm_cap, SUBLANES)
        tiles_per_seq = 0
        tail_rows = tm
        prev_spec = pl.BlockSpec((tm, tn), lambda i, j: (i, j))
    return pl.pallas_call(
        functools.partial(_ffn_up_body, tiles_per_seq),
        grid=(m // tm, nj),
        in_specs=[pl.BlockSpec((tm, k), lambda i, j: (i, 0)),
                  pl.BlockSpec((1, k), lambda i, j: (0, 0)),
                  pl.BlockSpec((k, tn), lambda i, j: (0, j)),
                  pl.BlockSpec((k, tn), lambda i, j: (0, nj + j)),
                  pl.BlockSpec((3, tn), lambda i, j: (0, j)),
                  pl.BlockSpec((1, tn), lambda i, j: (0, j)),
                  prev_spec, prev_spec],
        out_specs=[pl.BlockSpec((tm, tn), lambda i, j: (i, j)),
                   pl.BlockSpec((1, tail_rows, tn), lambda i, j: (i, 0, j))],
        out_shape=[jax.ShapeDtypeStruct((m, d_ff), BF16), jax.ShapeDtypeStruct((m // tm, tail_rows, d_ff), F32)],
        scratch_shapes=[pltpu.VMEM((tm, k), BF16), pltpu.VMEM((nj, SUBLANES, tn), F32)],
        compiler_params=_params(2),
        name="ffn_up",
    )(x, g.reshape(1, k).astype(F32), w_up, w_up, conv_w.astype(F32), conv_b.reshape(1, d_ff).astype(F32), prev1, prev2)


def _trunk(x, p, layer, dims, paged, hgrn_state, mem_k, mem_v, conv_buf):
    B, T, D = x.shape
    kvw, ih, idd, hgh, d_ff = dims["kvw"], dims["ih"], dims["idd"], dims["hgh"], dims["d_ff"]
    x2 = x.reshape(B * T, D)
    seg_widths = (D, 2 * kvw, ih * idd, 4 * D, 2 * D)
    tn = _tile(math.gcd(*seg_widths), 1024)
    q2, kv2, iq2, zb, zc, side = mix_in_proj(x2, p["g_pre_mix"], p["w_main"], p["w_side"], seg_widths,
                                             (BF16, F32, BF16, F32, F32), tn)
    k = kv2[:, :kvw].reshape(B, T, -1, dims["hd"])
    v = kv2[:, kvw:].reshape(B, T, -1, dims["hd"])
    ik = side[:, :idd].reshape(B, T, idd)
    prompt = paged is None
    cfg = dict(hd=dims["hd"], kvh=kvw // dims["hd"], ih=ih, idd=idd)
    if prompt:
        o_a = dsa_prompt(q2, kv2.astype(BF16), iq2, side, p["rel_bias"], B, T, cfg)
        o_b, s_new = hgrn_prompt(zb, p["hg_lb"], layer, p["hg_norm"], B, T, hgh)
    else:
        o_a = dsa_sample(q2, kv2, iq2, side, rel_bias=p["rel_bias"], cfg=cfg, **paged)
        o_b, s_new = hgrn_sample(zb, p["hg_lb"], layer, p["hg_norm"], hgrn_state, T)
    x2 = merge_out_proj(o_a, o_b, zc, p["w_out"], p["g_post_mix"], x2)
    xh = mem_k.shape[2]
    mlen = mem_k.shape[1]
    qx = norm_matmul(x2, p["g_pre_x"], p["w_xq"], BF16 if prompt else F32)
    ox = cross_attend(qx, mem_k.reshape(B, mlen, D), mem_v.reshape(B, mlen, D), B, T, xh, BF16 if prompt else F32)
    x2 = matmul_norm_res(ox, p["w_xo"], p["g_post_x"], x2)
    if prompt:
        y, tail = ffn_up(x2, p["g_pre_ffn"], p["w_up"], p["conv_w"], p["conv_b"], d_ff, T)
        tiles_per_seq = tail.shape[0] // B
        new_buf = tail[tiles_per_seq - 1::tiles_per_seq, SUBLANES - 2:, :]
    else:
        zeros = jnp.zeros((B, T - 2, d_ff), F32)
        prev1 = jnp.concatenate([conv_buf[:, 1:2], zeros, jnp.zeros((B, 1, d_ff), F32)], axis=1).reshape(B * T, d_ff)
        prev2 = jnp.concatenate([conv_buf, zeros], axis=1).reshape(B * T, d_ff)
        y, tail = ffn_up(x2, p["g_pre_ffn"], p["w_up"], p["conv_w"], p["conv_b"], d_ff, T, prev1, prev2)
        new_buf = tail.reshape(B, T, d_ff)[:, T - 2:, :]
    x2 = matmul_norm_res(y, p["w_down"], p["g_post_ffn"], x2)
    return x2.reshape(B, T, D), k, v, ik, s_new, new_buf


def kernel(x_prompt, x_sample, mem_prompt, cache_k, cache_v, cache_idx_k, cache_mem_k, cache_mem_v, state_hgrn, state_conv, page_table, rel_bias, hg_lb, g_pre_mix, w_in, hg_norm, w_out, g_post_mix, g_pre_x, g_mem, w_xq, w_xk, w_xv, w_xo, g_post_x, g_pre_ffn, w_up, conv_w, conv_b, w_down, g_post_ffn):
    depth = w_in.shape[0]
    B, S, D = x_prompt.shape
    kvh, hd = cache_k.shape[-2:]
    kvw = kvh * hd
    idd = cache_idx_k.shape[-1]
    mix = w_in.shape[-1]
    ih = (mix - 7 * D - 2 * kvw - idd) // (idd + 1)
    hgh = state_hgrn.shape[2]
    d_ff = w_down.shape[1]
    dims = dict(kvw=kvw, ih=ih, idd=idd, hgh=hgh, d_ff=d_ff, hd=hd)
    xp, xs = x_prompt, x_sample
    new = [[] for _ in range(12)]
    n_a = D + 2 * kvw + ih * idd
    pad = (-(ih + idd)) % LANE
    for l in range(depth):
        wl = w_in[l]
        w_main = jnp.concatenate([wl[:, :n_a], wl[:, n_a + ih + idd:]], axis=1).astype(BF16)
        w_side = jnp.concatenate([wl[:, n_a + ih:n_a + ih + idd], wl[:, n_a:n_a + ih], jnp.zeros((D, pad), F32)],
                                 axis=1).astype(BF16)
        p = dict(g_pre_mix=g_pre_mix[l], w_main=w_main, w_side=w_side, rel_bias=rel_bias, hg_lb=hg_lb, hg_norm=hg_norm[l],
                 w_out=w_out[l].astype(BF16),
                 g_post_mix=g_post_mix[l], g_pre_x=g_pre_x[l], w_xq=w_xq[l].astype(BF16), w_xo=w_xo[l].astype(BF16),
                 g_post_x=g_post_x[l], g_pre_ffn=g_pre_ffn[l], w_up=w_up[l].astype(BF16), conv_w=conv_w[l],
                 conv_b=conv_b[l], w_down=w_down[l].astype(BF16), g_post_ffn=g_post_ffn[l])
        mb, mm, _ = mem_prompt.shape
        xh, xd = cache_mem_k.shape[-2:]
        mem2 = mem_prompt.reshape(mb * mm, D)
        mk = norm_matmul(mem2, g_mem[l], w_xk[l].astype(BF16), F32).reshape(mb, mm, xh, xd)
        mv = norm_matmul(mem2, g_mem[l], w_xv[l].astype(BF16), F32).reshape(mb, mm, xh, xd)
        xp, kp, vp, ikp, sp, cp = _trunk(xp, p, l, dims, None, None, mk, mv, None)
        paged = dict(cache_k=cache_k[l], cache_v=cache_v[l], cache_idx_k=cache_idx_k[l], page_table=page_table)
        xs, ks, vs, iks, ss, cs = _trunk(xs, p, l, dims, paged, state_hgrn[l], cache_mem_k[l], cache_mem_v[l], state_conv[l])
        for store, val in zip(new, (kp, vp, ikp, ks, vs, iks, mk, mv, sp, ss, cp, cs)):
            store.append(val)
    stacked = [jnp.stack(s, axis=0) for s in new]
    return (xp, xs, *stacked)
```

```python
import functools
import math

import numpy as np
import jax
import jax.numpy as jnp
from jax import lax
from jax.experimental import pallas as pl
from jax.experimental.pallas import tpu as pltpu

F32 = jnp.float32
BF16 = jnp.bfloat16

EPS = 1e-6
TOPK_MAX = 256
MAX_DISTANCE = 128

V7X_VMEM_BYTES = 64 * 1024 * 1024
VMEM_LIMIT_BYTES = V7X_VMEM_BYTES - 8 * 1024 * 1024
LANE = 128


def _params(n_axes):
    return pltpu.CompilerParams(dimension_semantics=("arbitrary",) * n_axes, vmem_limit_bytes=VMEM_LIMIT_BYTES)


def _tile(n, cap, unit=LANE):
    if n <= cap:
        return n
    best = None
    for t in range(unit, cap + 1, unit):
        if n % t == 0:
            best = t
    assert best is not None, (n, cap, unit)
    return best


def _norm_matmul_body(x_ref, g_ref, w_ref, o_ref, h_ref):
    @pl.when(pl.program_id(1) == 0)
    def _():
        x = x_ref[...]
        ms = jnp.mean(x * x, axis=-1, keepdims=True)
        h_ref[...] = (x * lax.rsqrt(ms + EPS) * g_ref[...]).astype(BF16)

    o_ref[...] = jnp.dot(h_ref[...], w_ref[...], preferred_element_type=F32).astype(o_ref.dtype)


def norm_matmul(x, g, w, out_dtype, tm_cap=512, tn_cap=1024):
    m, k = x.shape
    n = w.shape[1]
    tm, tn = _tile(m, tm_cap, 8), _tile(n, tn_cap)
    return pl.pallas_call(
        _norm_matmul_body,
        grid=(m // tm, n // tn),
        in_specs=[pl.BlockSpec((tm, k), lambda i, j: (i, 0)),
                  pl.BlockSpec((1, k), lambda i, j: (0, 0)),
                  pl.BlockSpec((k, tn), lambda i, j: (0, j))],
        out_specs=pl.BlockSpec((tm, tn), lambda i, j: (i, j)),
        out_shape=jax.ShapeDtypeStruct((m, n), out_dtype),
        scratch_shapes=[pltpu.VMEM((tm, k), BF16)],
        compiler_params=_params(2),
        name="norm_matmul",
    )(x, g.reshape(1, k).astype(F32), w)


def _matmul_norm_res_body(a_ref, w_ref, g_ref, r_ref, o_ref, acc_ref):
    kk = pl.program_id(1)

    @pl.when(kk == 0)
    def _():
        acc_ref[...] = jnp.zeros_like(acc_ref)

    acc_ref[...] += jnp.dot(a_ref[...].astype(BF16), w_ref[...], preferred_element_type=F32)

    @pl.when(kk == pl.num_programs(1) - 1)
    def _():
        y = acc_ref[...]
        ms = jnp.mean(y * y, axis=-1, keepdims=True)
        o_ref[...] = r_ref[...] + y * lax.rsqrt(ms + EPS) * g_ref[...]


def matmul_norm_res(a, w, g, res, tm_cap=512, tk_cap=2048):
    m, k = a.shape
    n = w.shape[1]
    tm, tk = _tile(m, tm_cap, 8), _tile(k, tk_cap)
    return pl.pallas_call(
        _matmul_norm_res_body,
        grid=(m // tm, k // tk),
        in_specs=[pl.BlockSpec((tm, tk), lambda i, j: (i, j)),
                  pl.BlockSpec((tk, n), lambda i, j: (j, 0)),
                  pl.BlockSpec((1, n), lambda i, j: (0, 0)),
                  pl.BlockSpec((tm, n), lambda i, j: (i, 0))],
        out_specs=pl.BlockSpec((tm, n), lambda i, j: (i, 0)),
        out_shape=jax.ShapeDtypeStruct((m, n), F32),
        scratch_shapes=[pltpu.VMEM((tm, n), F32)],
        compiler_params=_params(2),
        name="matmul_norm_res",
    )(a, w, g.reshape(1, n).astype(F32), res)


def _mix_in_body(seg_tiles, x_ref, g_ref, w_ref, ws_ref, *rest):
    n_seg = len(seg_tiles)
    out_refs, side_ref, h_ref = rest[:n_seg], rest[n_seg], rest[n_seg + 1]
    j = pl.program_id(1)

    @pl.when(j == 0)
    def _():
        x = x_ref[...]
        ms = jnp.mean(x * x, axis=-1, keepdims=True)
        h = (x * lax.rsqrt(ms + EPS) * g_ref[...]).astype(BF16)
        h_ref[...] = h
        side_ref[...] = jnp.dot(h, ws_ref[...], preferred_element_type=F32)

    start = 0
    for o_ref, n_t in zip(out_refs, seg_tiles):
        @pl.when((j >= start) & (j < start + n_t))
        def _(o_ref=o_ref):
            o_ref[...] = jnp.dot(h_ref[...], w_ref[...], preferred_element_type=F32).astype(o_ref.dtype)
        start += n_t


def mix_in_proj(x, g, w_main, w_side, seg_widths, seg_dtypes, tn, tm_cap=512):
    m, k = x.shape
    tm = _tile(m, tm_cap, 8)
    seg_tiles = tuple(wd // tn for wd in seg_widths)
    starts = tuple(int(s) for s in np.cumsum((0,) + seg_tiles[:-1]))

    def seg_map(start, n_t):
        return lambda i, j: (i, jnp.clip(j - start, 0, n_t - 1))

    out_specs = [pl.BlockSpec((tm, tn), seg_map(s, n)) for s, n in zip(starts, seg_tiles)]
    out_specs.append(pl.BlockSpec((tm, LANE), lambda i, j: (i, 0)))
    out_shape = [jax.ShapeDtypeStruct((m, wd), dt) for wd, dt in zip(seg_widths, seg_dtypes)]
    out_shape.append(jax.ShapeDtypeStruct((m, LANE), F32))
    return pl.pallas_call(
        functools.partial(_mix_in_body, seg_tiles),
        grid=(m // tm, sum(seg_tiles)),
        in_specs=[pl.BlockSpec((tm, k), lambda i, j: (i, 0)),
                  pl.BlockSpec((1, k), lambda i, j: (0, 0)),
                  pl.BlockSpec((k, tn), lambda i, j: (0, j)),
                  pl.BlockSpec((k, LANE), lambda i, j: (0, 0))],
        out_specs=out_specs,
        out_shape=out_shape,
        scratch_shapes=[pltpu.VMEM((tm, k), BF16)],
        compiler_params=_params(2),
        name="mix_in_proj",
    )(x, g.reshape(1, k).astype(F32), w_main, w_side)


INT_MIN = -2 ** 31
MASK_NEG = -1e30
_NT = (((1,), (1,)), ((), ()))


def _sortable_key(x):
    i = pltpu.bitcast(x, jnp.int32)
    return i ^ (lax.shift_right_arithmetic(i, 31) & jnp.int32(0x7FFFFFFF))


def _bucket_thresholds(n_buckets, max_distance, d_max):
    max_exact = n_buckets // 2
    d = np.arange(d_max, dtype=np.int64)
    log_ratio = np.log(np.maximum(d, 1).astype(np.float32) / np.float32(max_exact)) / np.float32(math.log(max_distance / max_exact))
    large = np.minimum(max_exact + (log_ratio * np.float32(n_buckets - max_exact)).astype(np.int32), n_buckets - 1)
    bucket = np.where(d < max_exact, d, large)
    assert np.all(np.diff(bucket) >= 0)
    ths = [int(np.argmax(bucket >= max_exact + n)) for n in range(1, n_buckets - max_exact)]
    return max_exact, ths, bucket


def _kth_largest_threshold(count_ge, kf, shape):
    lo = jnp.where(count_ge(jnp.zeros(shape, jnp.int32)) >= kf, jnp.int32(0), jnp.int32(INT_MIN))

    def step(bi, lo):
        cand = lo + lax.shift_left(jnp.int32(1), 30 - bi)
        return jnp.where(count_ge(cand) >= kf, cand, lo)

    return lax.fori_loop(0, 31, step, lo)


def _dsa_prompt_body(cfg, relb_ref, q_ref, kv_ref, iq_ref, sq_ref, sk_ref, o_ref,
                     ikl_scr, ikh_scr, keys_scr, madd_scr, m_scr, l_scr, acc_scr, bias_scr):
    tq, hd, kvh, qpk, ih, idd, topk, n_buckets = (cfg[k] for k in ("tq", "hd", "kvh", "qpk", "ih", "idd", "topk", "n_buckets"))
    kvw = kvh * hd
    nl = tq // LANE
    rows = qpk * tq
    scale = hd ** -0.5
    b, i = pl.program_id(0), pl.program_id(1)

    @pl.when((b == 0) & (i == 0))
    def _():
        max_exact, ths, _ = _bucket_thresholds(n_buckets, MAX_DISTANCE, 2 * tq)
        tl = lax.broadcasted_iota(jnp.int32, (tq, tq), 0)
        sl = lax.broadcasted_iota(jnp.int32, (tq, tq), 1)
        for r in range(2):
            d = jnp.maximum(r * tq + tl - sl, 0)
            bucket = jnp.minimum(d, max_exact)
            for th in ths:
                bucket = bucket + jnp.where(d >= th, 1, 0)
            for h in range(kvh * qpk):
                far = relb_ref[n_buckets - 1, h]
                val = jnp.full((tq, tq), far, F32)
                for bk in range(n_buckets - 1):
                    val = jnp.where(bucket == bk, relb_ref[bk, h], val)
                g, hl = divmod(h, qpk)
                bias_scr[g, r, hl * tq:(hl + 1) * tq, :] = (val - far) * (1.0 / scale)

    @pl.when(i == 0)
    def _():
        side = sk_ref[...]
        lane = lax.broadcasted_iota(jnp.int32, side.shape, 1)
        lo = jnp.where(lane < idd, side, 0.0)
        ikl_scr[...] = lo.astype(BF16)
        ikh_scr[...] = pltpu.roll(lo, idd, axis=1).astype(BF16)

    w_rows = sq_ref[...].T[idd:idd + ih, :] * (ih ** -0.5 * idd ** -0.5)
    t_pos = i * tq + lax.broadcasted_iota(jnp.int32, (tq, tq), 1)
    s_loc = lax.broadcasted_iota(jnp.int32, (tq, tq), 0)

    def index_chunk(c, carry):
        r0 = pl.multiple_of(c * tq, tq)
        kl = ikl_scr[pl.ds(r0, tq), :]
        kh = ikh_scr[pl.ds(r0, tq), :]
        acc = jnp.zeros((tq, tq), F32)
        for jp in range(ih // 2):
            iqp = iq_ref[:, jp * LANE:(jp + 1) * LANE]
            x0 = lax.dot_general(kl, iqp, _NT, preferred_element_type=F32)
            x1 = lax.dot_general(kh, iqp, _NT, preferred_element_type=F32)
            acc = acc + jnp.maximum(x0, 0.0) * w_rows[2 * jp:2 * jp + 1, :] + jnp.maximum(x1, 0.0) * w_rows[2 * jp + 1:2 * jp + 2, :]
        keys_scr[c] = jnp.where(c * tq + s_loc <= t_pos, _sortable_key(acc), jnp.int32(INT_MIN))
        return carry

    lax.fori_loop(0, i + 1, index_chunk, 0)

    def count_ge(cand):
        def chunk(c, acc):
            hit = jnp.where(keys_scr[c] >= cand, 1.0, 0.0)
            return acc + jnp.sum(hit.reshape(tq // 8, 8, tq), axis=0)
        acc = lax.fori_loop(0, i + 1, chunk, jnp.zeros((8, tq), F32))
        return jnp.sum(acc, axis=0, keepdims=True)

    thr = _kth_largest_threshold(count_ge, float(topk), (1, tq))
    thr = jnp.maximum(thr, jnp.int32(INT_MIN + 1))

    n_ge = count_ge(thr)
    has_ties = jnp.max(n_ge) > float(topk)

    @pl.when(jnp.logical_not(has_ties))
    def _():
        def mask_chunk(c, carry):
            madd_scr[c] = jnp.where(keys_scr[c] >= thr, 0.0, MASK_NEG).T
            return carry

        lax.fori_loop(0, i + 1, mask_chunk, 0)

    @pl.when(has_ties)
    def _():
        need = float(topk) - count_ge(thr + 1)
        lower_tri = jnp.where(lax.broadcasted_iota(jnp.int32, (tq, tq), 0) >= lax.broadcasted_iota(jnp.int32, (tq, tq), 1),
                              1.0, 0.0).astype(BF16)

        def mask_chunk(c, seen):
            kc = keys_scr[c]
            tie = jnp.where(kc == thr, 1.0, 0.0)
            rank = seen + jnp.dot(lower_tri, tie.astype(BF16), preferred_element_type=F32)
            take = (kc > thr) | ((kc == thr) & (rank <= need))
            madd_scr[c] = jnp.where(take, 0.0, MASK_NEG).T
            return seen + jnp.sum(tie, axis=0, keepdims=True)

        lax.fori_loop(0, i + 1, mask_chunk, jnp.zeros((1, tq), F32))

    cexp = scale * math.log2(math.e)

    def fold(x, op):
        y = x[:, :LANE]
        for t in range(1, nl):
            y = op(y, x[:, t * LANE:(t + 1) * LANE])
        return y

    m_scr[...] = jnp.full(m_scr.shape, -jnp.inf, F32)
    l_scr[...] = jnp.zeros(l_scr.shape, F32)
    acc_scr[...] = jnp.zeros(acc_scr.shape, F32)

    def attend_chunk(c, bias_r):
        r0 = pl.multiple_of(c * tq, tq)
        md = jnp.concatenate([madd_scr[c]] * qpk, axis=0)
        for g in range(kvh):
            qg = jnp.concatenate([q_ref[:, (g * qpk + hl) * hd:(g * qpk + hl + 1) * hd] for hl in range(qpk)], axis=0)
            kc = kv_ref[pl.ds(r0, tq), g * hd:(g + 1) * hd]
            z = lax.dot_general(qg, kc, _NT, preferred_element_type=F32) + md
            if bias_r is not None:
                z = z + bias_scr[g, bias_r]
            m_old = m_scr[g]
            m_new = jnp.maximum(m_old, jnp.max(fold(z, jnp.maximum), axis=1, keepdims=True))
            alpha = jnp.exp2((m_old - m_new) * cexp)
            p = jnp.exp2((z - jnp.concatenate([m_new] * nl, axis=1)) * cexp)
            m_scr[g] = m_new
            l_scr[g] = alpha * l_scr[g] + fold(p, jnp.add)
            vc = kv_ref[pl.ds(r0, tq), kvw + g * hd:kvw + (g + 1) * hd]
            pv = jnp.dot(p.astype(BF16), vc, preferred_element_type=F32)
            acc_scr[g] = jnp.concatenate([alpha] * (hd // LANE), axis=1) * acc_scr[g] + pv

    def far_chunk(c, carry):
        attend_chunk(c, None)
        return carry

    lax.fori_loop(0, jnp.maximum(i - 1, 0), far_chunk, 0)

    @pl.when(i >= 1)
    def _():
        attend_chunk(i - 1, 1)

    attend_chunk(i, 0)

    for g in range(kvh):
        o = acc_scr[g] / jnp.sum(l_scr[g], axis=1, keepdims=True)
        for hl in range(qpk):
            h = g * qpk + hl
            o_ref[:, h * hd:(h + 1) * hd] = o[hl * tq:(hl + 1) * tq, :].astype(o_ref.dtype)


def dsa_prompt(q, kv, iq, side, rel_bias, batch, seq, cfg, tq_cap=256):
    m, a_width = q.shape
    hd, kvh, ih, idd = cfg["hd"], cfg["kvh"], cfg["ih"], cfg["idd"]
    n_heads = a_width // hd
    qpk = n_heads // kvh
    tq = _tile(seq, tq_cap)
    n_buckets = rel_bias.shape[0]
    assert 2 * idd == LANE and ih % 2 == 0 and idd + ih <= LANE and seq % tq == 0
    _, _, bucket = _bucket_thresholds(n_buckets, MAX_DISTANCE, 2 * tq)
    assert bucket[tq + 1] == n_buckets - 1, "bias must be saturated two key chunks away from the diagonal"
    nq = seq // tq
    kcfg = dict(tq=tq, hd=hd, kvh=kvh, qpk=qpk, ih=ih, idd=idd, topk=min(TOPK_MAX, seq // 4), n_buckets=n_buckets)
    rows = qpk * tq
    return pl.pallas_call(
        functools.partial(_dsa_prompt_body, kcfg),
        grid=(batch, nq),
        in_specs=[pl.BlockSpec(memory_space=pltpu.SMEM),
                  pl.BlockSpec((tq, a_width), lambda b, i: (b * nq + i, 0)),
                  pl.BlockSpec((seq, kv.shape[1]), lambda b, i: (b, 0)),
                  pl.BlockSpec((tq, iq.shape[1]), lambda b, i: (b * nq + i, 0)),
                  pl.BlockSpec((tq, LANE), lambda b, i: (b * nq + i, 0)),
                  pl.BlockSpec((seq, LANE), lambda b, i: (b, 0))],
        out_specs=pl.BlockSpec((tq, a_width), lambda b, i: (b * nq + i, 0)),
        out_shape=jax.ShapeDtypeStruct((m, a_width), F32),
        scratch_shapes=[pltpu.VMEM((seq, LANE), BF16), pltpu.VMEM((seq, LANE), BF16),
                        pltpu.VMEM((nq, tq, tq), jnp.int32), pltpu.VMEM((nq, tq, tq), F32),
                        pltpu.VMEM((kvh, rows, LANE), F32), pltpu.VMEM((kvh, rows, LANE), F32),
                        pltpu.VMEM((kvh, rows, hd), F32), pltpu.VMEM((kvh, 2, rows, tq), F32)],
        compiler_params=_params(2),
        name="dsa_prompt",
    )(rel_bias.astype(F32), q, kv, iq, side, side)


def _paged_specs(n_pages, block, layer):
    return [pl.BlockSpec((1, 1) + block, lambda b, pt, p=p: (layer, pt[b, p]) + (0,) * len(block)) for p in range(n_pages)]


def _dsa_sample_scores_body(cfg, pt_ref, iq_ref, iw_ref, *rest):
    n_pages, ih, idd, t_seq = (cfg[k] for k in ("n_pages", "ih", "idd", "t_seq"))
    page_refs, new_ref, keys_ref = rest[:n_pages], rest[n_pages], rest[n_pages + 1]
    iq = iq_ref[0]
    w = iw_ref[0] * (ih ** -0.5 * idd ** -0.5)
    page = new_ref.shape[1]
    t = lax.broadcasted_iota(jnp.int32, (t_seq, page), 0)
    j = lax.broadcasted_iota(jnp.int32, (t_seq, page), 1)
    for c in range(n_pages + 1):
        ikc = (page_refs[c][0, 0] if c < n_pages else new_ref[0]).astype(BF16)
        x = lax.dot_general(iq, ikc, _NT, preferred_element_type=F32)
        sc = jnp.sum((jnp.maximum(x, 0.0) * w).reshape(ih, t_seq, page), axis=0)
        key = _sortable_key(sc)
        if c == n_pages:
            key = jnp.where(j <= t, key, jnp.int32(INT_MIN))
        keys_ref[0, c] = key


def _dsa_sample_mask_body(topk, keys_ref, madd_ref):
    nb, n_chunks, t_seq, page = keys_ref.shape
    rows = nb * t_seq

    def chunk(c):
        return keys_ref[:, c].reshape(rows, page)

    def count_ge(cand):
        acc = jnp.zeros((rows, page), F32)
        for c in range(n_chunks):
            acc = acc + jnp.where(chunk(c) >= cand, 1.0, 0.0)
        return jnp.sum(acc, axis=1, keepdims=True)

    thr = _kth_largest_threshold(count_ge, float(topk), (rows, 1))
    thr = jnp.maximum(thr, jnp.int32(INT_MIN + 1))
    has_ties = jnp.max(count_ge(thr)) > float(topk)

    @pl.when(jnp.logical_not(has_ties))
    def _():
        for c in range(n_chunks):
            madd_ref[:, c] = jnp.where(chunk(c) >= thr, 0.0, MASK_NEG).reshape(nb, t_seq, page)

    @pl.when(has_ties)
    def _():
        need = float(topk) - count_ge(thr + 1)
        upper_tri = jnp.where(lax.broadcasted_iota(jnp.int32, (page, page), 0) <= lax.broadcasted_iota(jnp.int32, (page, page), 1),
                              1.0, 0.0).astype(BF16)
        seen = jnp.zeros((rows, 1), F32)
        for c in range(n_chunks):
            kc = chunk(c)
            tie = jnp.where(kc == thr, 1.0, 0.0)
            rank = seen + jnp.dot(tie.astype(BF16), upper_tri, preferred_element_type=F32)
            take = (kc > thr) | ((kc == thr) & (rank <= need))
            madd_ref[:, c] = jnp.where(take, 0.0, MASK_NEG).reshape(nb, t_seq, page)
            seen = seen + jnp.sum(tie, axis=1, keepdims=True)


def _dsa_sample_attend_body(cfg, pt_ref, relb_ref, q_ref, madd_ref, *rest):
    n_pages, hd, kvh, qpk, t_seq, n_buckets = (cfg[k] for k in ("n_pages", "hd", "kvh", "qpk", "t_seq", "n_buckets"))
    k_refs, v_refs = rest[:n_pages], rest[n_pages:2 * n_pages]
    new_ref, o_ref, s_scr, bias_scr = rest[2 * n_pages:]
    n_chunks = n_pages + 1
    n_heads = kvh * qpk
    kvw = kvh * hd
    rows = n_heads * t_seq
    grows = qpk * t_seq
    page = new_ref.shape[1]
    scale = hd ** -0.5

    @pl.when(pl.program_id(0) == 0)
    def _():
        max_exact, ths, _ = _bucket_thresholds(n_buckets, MAX_DISTANCE, 2 * page)
        t = lax.broadcasted_iota(jnp.int32, (t_seq, page), 0)
        j = lax.broadcasted_iota(jnp.int32, (t_seq, page), 1)
        for r in range(2):
            d = jnp.maximum(r * page + t - j, 0)
            bucket = jnp.minimum(d, max_exact)
            for th in ths:
                bucket = bucket + jnp.where(d >= th, 1, 0)
            for h in range(n_heads):
                far = relb_ref[n_buckets - 1, h]
                val = jnp.full((t_seq, page), far, F32)
                for bk in range(n_buckets - 1):
                    val = jnp.where(bucket == bk, relb_ref[bk, h], val)
                bias_scr[r, h * t_seq:(h + 1) * t_seq, :] = (val - far) * (1.0 / scale)

    q = q_ref[0]
    m = jnp.full((rows, page), -jnp.inf, F32)
    def head_rows(c, g, refs, new_off):
        if c < n_pages:
            return refs[c][0, 0, :, g, :].astype(BF16)
        return new_ref[0, :, new_off + g * hd:new_off + (g + 1) * hd].astype(BF16)

    for c in range(n_chunks):
        z = jnp.concatenate([lax.dot_general(q[g * grows:(g + 1) * grows, :], head_rows(c, g, k_refs, 0), _NT,
                                             preferred_element_type=F32) for g in range(kvh)], axis=0)
        z = z + jnp.concatenate([madd_ref[0, c]] * n_heads, axis=0)
        if c >= n_chunks - 2:
            z = z + bias_scr[n_chunks - 1 - c]
        s_scr[c] = z
        m = jnp.maximum(m, z)
    m_row = jnp.max(m, axis=1, keepdims=True)
    cexp = scale * math.log2(math.e)
    l_acc = jnp.zeros((rows, page), F32)
    acc = [jnp.zeros((grows, hd), F32) for _ in range(kvh)]
    for c in range(n_chunks):
        p = jnp.exp2((s_scr[c] - m_row) * cexp)
        l_acc = l_acc + p
        pb = p.astype(BF16)
        for g in range(kvh):
            acc[g] = acc[g] + jnp.dot(pb[g * grows:(g + 1) * grows, :], head_rows(c, g, v_refs, kvw), preferred_element_type=F32)
    l_row = jnp.sum(l_acc, axis=1, keepdims=True)
    o_ref[0] = jnp.concatenate(acc, axis=0) / l_row


def dsa_sample(q, kv, iq, side, cache_k, cache_v, cache_idx_k, layer, page_table, rel_bias, cfg, seqs_per_mask_step=16):
    hd, kvh, ih, idd = cfg["hd"], cfg["kvh"], cfg["ih"], cfg["idd"]
    n_seq, n_pages = page_table.shape
    page = cache_k.shape[2]
    m, a_width = q.shape
    t_seq = m // n_seq
    n_heads = a_width // hd
    qpk = n_heads // kvh
    kvw = kvh * hd
    n_chunks = n_pages + 1
    n_buckets = rel_bias.shape[0]
    assert t_seq == SUBLANES and page == LANE
    _, _, bucket = _bucket_thresholds(n_buckets, MAX_DISTANCE, 2 * page)
    assert bucket[page + 1] == n_buckets - 1, "bias must be saturated two pages before the new tokens"
    topk = min(TOPK_MAX, (n_pages * page + t_seq) // 4)
    iq_ht = iq.reshape(n_seq, t_seq, ih, idd).transpose(0, 2, 1, 3).reshape(n_seq, ih * t_seq, idd)
    iw_ht = side[:, idd:idd + ih].reshape(n_seq, t_seq, ih).transpose(0, 2, 1).reshape(n_seq, ih * t_seq, 1)
    q_ht = q.reshape(n_seq, t_seq, n_heads, hd).transpose(0, 2, 1, 3).reshape(n_seq, n_heads * t_seq, hd)
    pad_rows = ((0, 0), (0, page - t_seq), (0, 0))
    ik_new = jnp.pad(side[:, :idd].reshape(n_seq, t_seq, idd), pad_rows)
    kv_new = jnp.pad(kv.reshape(n_seq, t_seq, 2 * kvw), pad_rows)
    per_seq = lambda shape: pl.BlockSpec((1,) + shape, lambda b, pt: (b,) + (0,) * len(shape))

    scfg = dict(n_pages=n_pages, ih=ih, idd=idd, t_seq=t_seq)
    keys = pl.pallas_call(
        functools.partial(_dsa_sample_scores_body, scfg),
        grid_spec=pltpu.PrefetchScalarGridSpec(
            num_scalar_prefetch=1, grid=(n_seq,),
            in_specs=[per_seq((ih * t_seq, idd)), per_seq((ih * t_seq, 1))]
                     + _paged_specs(n_pages, (page, idd), layer) + [per_seq((page, idd))],
            out_specs=per_seq((n_chunks, t_seq, page))),
        out_shape=jax.ShapeDtypeStruct((n_seq, n_chunks, t_seq, page), jnp.int32),
        compiler_params=_params(1),
        name="dsa_sample_scores",
    )(page_table, iq_ht, iw_ht, *([cache_idx_k] * n_pages), ik_new)

    nb = _tile(n_seq, seqs_per_mask_step, 1)
    blk = pl.BlockSpec((nb, n_chunks, t_seq, page), lambda i: (i, 0, 0, 0))
    madd = pl.pallas_call(
        functools.partial(_dsa_sample_mask_body, topk),
        grid=(n_seq // nb,),
        in_specs=[blk],
        out_specs=blk,
        out_shape=jax.ShapeDtypeStruct((n_seq, n_chunks, t_seq, page), F32),
        compiler_params=_params(1),
        name="dsa_sample_mask",
    )(keys)

    acfg = dict(n_pages=n_pages, hd=hd, kvh=kvh, qpk=qpk, t_seq=t_seq, n_buckets=n_buckets)
    o_ht = pl.pallas_call(
        functools.partial(_dsa_sample_attend_body, acfg),
        grid_spec=pltpu.PrefetchScalarGridSpec(
            num_scalar_prefetch=1, grid=(n_seq,),
            in_specs=[pl.BlockSpec(memory_space=pltpu.SMEM), per_seq((n_heads * t_seq, hd)), per_seq((n_chunks, t_seq, page))]
                     + 2 * _paged_specs(n_pages, (page, kvh, hd), layer) + [per_seq((page, 2 * kvw))],
            out_specs=per_seq((n_heads * t_seq, hd)),
            scratch_shapes=[pltpu.VMEM((n_chunks, n_heads * t_seq, page), F32),
                            pltpu.VMEM((2, n_heads * t_seq, page), F32)]),
        out_shape=jax.ShapeDtypeStruct((n_seq, n_heads * t_seq, hd), F32),
        compiler_params=_params(1),
        name="dsa_sample_attend",
    )(page_table, rel_bias.astype(F32), q_ht, madd, *([cache_k] * n_pages), *([cache_v] * n_pages), kv_new)
    return o_ht.reshape(n_seq, n_heads, t_seq, hd).transpose(0, 2, 1, 3).reshape(m, a_width)


SUBLANES = 8
GROUP_EXP_CLIP = 40.0
_TN = (((0,), (0,)), ((), ()))


def _split3_bf16(x):
    h1 = x.astype(BF16)
    r1 = x - h1.astype(F32)
    h2 = r1.astype(BF16)
    h3 = (r1 - h2.astype(F32)).astype(BF16)
    return jnp.concatenate([h1, h2, h3], axis=1)


def _block_row(b, m, row):
    c, w = b.shape
    b3 = b.reshape(c // m, m, w)
    return jnp.broadcast_to(b3[:, row:row + 1, :], (c // m, m, w)).reshape(c, w)


def _lower_bound(lb_logits, layer):
    e = jnp.exp(lb_logits - jnp.max(lb_logits, axis=0, keepdims=True))
    sm = e / jnp.sum(e, axis=0, keepdims=True)
    return jnp.sum(sm[1:layer + 2, :], axis=0, keepdims=True)


def _hgrn_gates(hf, lb):
    f = lb + (1.0 - lb) * jax.nn.sigmoid(hf)
    return jnp.log(f), 1.0 - f


def _hgrn_intra(qs_, kks_, bs_, block, level_of_pair):
    n = len(qs_)
    c = qs_[0].shape[0]
    a = [jnp.zeros((c, c), F32) for _ in range(n)]
    m = block
    while m >= SUBLANES:
        top = 0.0 if m > SUBLANES else GROUP_EXP_CLIP
        ref_row = [_block_row(b, m, m // 2 - 1) for b in bs_]
        qs = [(qs_[i] * jnp.exp(jnp.minimum(bs_[i] - ref_row[i], top))).astype(BF16) for i in range(n)]
        ks = [(kks_[i] * jnp.exp(jnp.minimum(ref_row[i] - bs_[i], top))).astype(BF16) for i in range(n)]
        pr = [lax.dot_general(qs[i], ks[i], _NT, preferred_element_type=F32) for i in range(n)]
        a = [jnp.where(level_of_pair == m, pr[i], a[i]) for i in range(n)]
        m //= 2
    return a


def _pair_levels(c, block):
    t = lax.broadcasted_iota(jnp.int32, (c, c), 0)
    s = lax.broadcasted_iota(jnp.int32, (c, c), 1)
    x = t ^ s
    lvl = jnp.full((c, c), SUBLANES, jnp.int32)
    m = SUBLANES
    while m < block:
        lvl = jnp.where(x >= m, 2 * m, lvl)
        m *= 2
    return jnp.where((s <= t) & (x < block), lvl, 0)


def _head_norm_gate(o, gn, hg):
    y = o * lax.rsqrt(jnp.mean(o * o, axis=-1, keepdims=True) + EPS) * gn
    return y * (hg * jax.nn.sigmoid(hg))


def _hgrn_prompt_body(chunk, layer, hps, hq_ref, hf_ref, hi_ref, hg_ref, lb_ref, gn_ref, o_ref, s_ref, st_scr):
    seq = hq_ref.shape[0]
    dk = hq_ref.shape[1] // hps
    c = chunk
    lb, gn = _lower_bound(lb_ref[...], layer), gn_ref[...]
    t = lax.broadcasted_iota(jnp.int32, (c, c), 0)
    s = lax.broadcasted_iota(jnp.int32, (c, c), 1)
    cum = jnp.where(s <= t, 1.0, 0.0).astype(BF16)
    levels = _pair_levels(c, c)
    st_scr[...] = jnp.zeros_like(st_scr)

    def step(ci, carry):
        r0 = pl.multiple_of(ci * c, c)
        rows = pl.ds(r0, c)
        hs = range(hps)
        cols = [slice(hh * dk, (hh + 1) * dk) for hh in hs]
        q = [hq_ref[rows, cs] for cs in cols]
        v = [hi_ref[rows, cs].astype(BF16) for cs in cols]
        gates = [_hgrn_gates(hf_ref[rows, cs], lb[:, cs]) for cs in cols]
        b3 = [jnp.dot(cum, _split3_bf16(g[0]), preferred_element_type=F32) for g in gates]
        b = [x[:, :dk] + x[:, dk:2 * dk] + x[:, 2 * dk:] for x in b3]
        kk = [g[1] for g in gates]
        a = _hgrn_intra(q, kk, b, c, levels)
        st = [st_scr[hh] for hh in hs]
        o = [jnp.dot(a[hh].astype(BF16), v[hh], preferred_element_type=F32) for hh in hs]
        qd = [(q[hh] * jnp.exp(b[hh])).astype(BF16) for hh in hs]
        o = [o[hh] + lax.dot_general(qd[hh], st[hh].astype(BF16), _NT, preferred_element_type=F32) for hh in hs]
        b_end = [b[hh][c - 1:c, :] for hh in hs]
        kd = [(kk[hh] * jnp.exp(b_end[hh] - b[hh])).astype(BF16) for hh in hs]
        for hh in hs:
            st_scr[hh] = st[hh] * jnp.exp(b_end[hh]) + lax.dot_general(v[hh], kd[hh], _TN, preferred_element_type=F32)
        for hh in hs:
            o_ref[rows, cols[hh]] = _head_norm_gate(o[hh], gn[:, cols[hh]], hg_ref[rows, cols[hh]])
        return carry

    lax.fori_loop(0, seq // c, step, 0)
    for hh in range(hps):
        s_ref[0, hh] = st_scr[hh].T


def hgrn_prompt(hz, hg_lb, layer, hg_norm, batch, seq, heads, chunk=128, heads_per_step=4):
    m, d4 = hz.shape
    d = d4 // 4
    dk = d // heads
    hps = heads_per_step
    assert dk == LANE and seq % chunk == 0 and heads % hps == 0
    hg_steps = heads // hps
    spec = lambda part: pl.BlockSpec((seq, hps * dk), lambda b, h: (b, part * hg_steps + h))
    vec = pl.BlockSpec((1, hps * dk), lambda b, h: (0, h))
    lb_spec = pl.BlockSpec((hg_lb.shape[0], hps * dk), lambda b, h: (0, h))
    return pl.pallas_call(
        functools.partial(_hgrn_prompt_body, chunk, layer, hps),
        grid=(batch, hg_steps),
        in_specs=[spec(0), spec(1), spec(2), spec(3), lb_spec, vec],
        out_specs=[pl.BlockSpec((seq, hps * dk), lambda b, h: (b, h)),
                   pl.BlockSpec((1, hps, dk, dk), lambda b, h: (b, h, 0, 0))],
        out_shape=[jax.ShapeDtypeStruct((m, d), F32), jax.ShapeDtypeStruct((batch, heads, dk, dk), F32)],
        scratch_shapes=[pltpu.VMEM((hps, dk, dk), F32)],
        compiler_params=_params(2),
        name="hgrn_prompt",
    )(hz, hz, hz, hz, hg_lb.astype(F32), hg_norm.reshape(1, d).astype(F32))


def _hgrn_sample_body(t_seq, layer, hq_ref, hf_ref, hi_ref, hg_ref, lb_ref, gn_ref, s0_ref, o_ref, s_ref):
    c, dk = hq_ref.shape
    nb = c // t_seq
    lb, gn = _lower_bound(lb_ref[...], layer), gn_ref[...]
    t = lax.broadcasted_iota(jnp.int32, (c, c), 0)
    s = lax.broadcasted_iota(jnp.int32, (c, c), 1)
    same_seq = (t ^ s) < t_seq
    cum = jnp.where((s <= t) & same_seq, 1.0, 0.0).astype(BF16)
    levels = _pair_levels(c, t_seq)
    q, v = hq_ref[...], hi_ref[...]
    logf, kk = _hgrn_gates(hf_ref[...], lb)
    b3 = jnp.dot(cum, _split3_bf16(logf), preferred_element_type=F32)
    b = b3[:, :dk] + b3[:, dk:2 * dk] + b3[:, 2 * dk:]
    a = _hgrn_intra([q], [kk], [b], t_seq, levels)[0]
    o = jnp.dot(a.astype(BF16), v.astype(BF16), preferred_element_type=F32)
    qd = (q * jnp.exp(b)).astype(BF16)
    b_end_rows = _block_row(b, t_seq, t_seq - 1)
    kd = (kk * jnp.exp(b_end_rows - b)).astype(BF16)
    row = lax.broadcasted_iota(jnp.int32, (c, dk), 0)
    for n in range(nb):
        in_seq = (row >= n * t_seq) & (row < (n + 1) * t_seq)
        st = s0_ref[n, 0].T
        o = o + jnp.where(in_seq, lax.dot_general(qd, st.astype(BF16), _NT, preferred_element_type=F32), 0.0)
        b_end = b[(n + 1) * t_seq - 1:(n + 1) * t_seq, :]
        vn = jnp.where(in_seq, v, 0.0).astype(BF16)
        st_new = st * jnp.exp(b_end) + lax.dot_general(vn, kd, _TN, preferred_element_type=F32)
        s_ref[n, 0] = st_new.T
    o_ref[...] = _head_norm_gate(o, gn, hg_ref[...])


def hgrn_sample(hz, hg_lb, layer, hg_norm, state, t_seq, rows_per_step=128):
    m, d4 = hz.shape
    d = d4 // 4
    n_seq, heads, dk, dv = state.shape
    assert dk == LANE and dv == LANE and t_seq == SUBLANES and m == n_seq * t_seq
    c = _tile(m, rows_per_step, t_seq)
    nb = c // t_seq
    spec = lambda part: pl.BlockSpec((c, dk), lambda i, h: (i, part * heads + h))
    vec = pl.BlockSpec((1, dk), lambda i, h: (0, h))
    st_spec = pl.BlockSpec((nb, 1, dk, dv), lambda i, h: (i, h, 0, 0))
    lb_spec = pl.BlockSpec((hg_lb.shape[0], dk), lambda i, h: (0, h))
    return pl.pallas_call(
        functools.partial(_hgrn_sample_body, t_seq, layer),
        grid=(m // c, heads),
        in_specs=[spec(0), spec(1), spec(2), spec(3), lb_spec, vec, st_spec],
        out_specs=[pl.BlockSpec((c, dk), lambda i, h: (i, h)), st_spec],
        out_shape=[jax.ShapeDtypeStruct((m, d), F32), jax.ShapeDtypeStruct(state.shape, F32)],
        compiler_params=_params(2),
        name="hgrn_sample",
    )(hz, hz, hz, hz, hg_lb.astype(F32), hg_norm.reshape(1, d).astype(F32), state)


def _merge_out_body(oa_ref, ob_ref, ga_ref, gb_ref, w_ref, g_ref, r_ref, o_ref):
    u = jax.nn.sigmoid(ga_ref[...]) * oa_ref[...] + jax.nn.sigmoid(gb_ref[...]) * ob_ref[...]
    y = jnp.dot(u.astype(BF16), w_ref[...], preferred_element_type=F32)
    ms = jnp.mean(y * y, axis=-1, keepdims=True)
    o_ref[...] = r_ref[...] + y * lax.rsqrt(ms + EPS) * g_ref[...]


def merge_out_proj(o_a, o_b, gz, w, g, res, tm_cap=256):
    m, d = o_a.shape
    n = w.shape[1]
    tm = _tile(m, tm_cap, 8)
    row = lambda i: (i, 0)
    return pl.pallas_call(
        _merge_out_body,
        grid=(m // tm,),
        in_specs=[pl.BlockSpec((tm, d), row), pl.BlockSpec((tm, d), row),
                  pl.BlockSpec((tm, d), row), pl.BlockSpec((tm, d), lambda i: (i, 1)),
                  pl.BlockSpec((d, n), lambda i: (0, 0)), pl.BlockSpec((1, n), lambda i: (0, 0)),
                  pl.BlockSpec((tm, n), row)],
        out_specs=pl.BlockSpec((tm, n), row),
        out_shape=jax.ShapeDtypeStruct((m, n), F32),
        compiler_params=_params(1),
        name="merge_out_proj",
    )(o_a, o_b, gz, gz, w, g.reshape(1, n).astype(F32), res)


def _xattn_body(q_ref, mk_ref, mv_ref, o_ref):
    xh, xd = mk_ref.shape[-2:]
    for h in range(xh):
        cols = slice(h * xd, (h + 1) * xd)
        qh = q_ref[:, cols].astype(BF16)
        s = lax.dot_general(qh, mk_ref[0, 0, :, h, :].astype(BF16), _NT, preferred_element_type=F32) * (xd ** -0.5)
        p = jnp.exp(s - jnp.max(s, axis=-1, keepdims=True))
        l_row = jnp.sum(p, axis=-1, keepdims=True)
        o = jnp.dot(p.astype(BF16), mv_ref[0, 0, :, h, :].astype(BF16), preferred_element_type=F32)
        o_ref[:, cols] = (o / l_row).astype(o_ref.dtype)


def cross_attend(q, mem_k, mem_v, layer, batch, t_seq, out_dtype, tq_cap=512):
    m, d = q.shape
    mlen, xh, xd = mem_k.shape[2:]
    tq = _tile(t_seq, tq_cap, 8)
    nq = t_seq // tq
    mem_spec = pl.BlockSpec((1, 1, mlen, xh, xd), lambda b, i: (layer, b, 0, 0, 0))
    return pl.pallas_call(
        _xattn_body,
        grid=(batch, nq),
        in_specs=[pl.BlockSpec((tq, d), lambda b, i: (b * nq + i, 0)), mem_spec, mem_spec],
        out_specs=pl.BlockSpec((tq, d), lambda b, i: (b * nq + i, 0)),
        out_shape=jax.ShapeDtypeStruct((m, d), out_dtype),
        compiler_params=_params(2),
        name="cross_attend",
    )(q, mem_k, mem_v)


def _gelu_tanh(x):
    return 0.5 * x * (1.0 + jnp.tanh(math.sqrt(2.0 / math.pi) * (x + 0.044715 * x * x * x)))


def _ffn_up_body(tiles_per_seq, x_ref, g_ref, wa_ref, wb_ref, cw_ref, cb_ref, p1_ref, p2_ref, y_ref, tail_ref, h_ref, halo_ref):
    i, j = pl.program_id(0), pl.program_id(1)
    tm, tn = y_ref.shape

    @pl.when(j == 0)
    def _():
        x = x_ref[...]
        ms = jnp.mean(x * x, axis=-1, keepdims=True)
        h_ref[...] = (x * lax.rsqrt(ms + EPS) * g_ref[...]).astype(BF16)

    h = h_ref[...]
    a = jnp.dot(h, wa_ref[...], preferred_element_type=F32)
    b = jnp.dot(h, wb_ref[...], preferred_element_type=F32)
    row = lax.broadcasted_iota(jnp.int32, (tm, tn), 0)
    if tiles_per_seq:
        @pl.when(i % tiles_per_seq == 0)
        def _():
            halo_ref[j] = jnp.zeros((SUBLANES, tn), F32)

        halo = halo_ref[j]
        a1 =jnp.where(row == 0, halo[SUBLANES - 1:SUBLANES, :], pltpu.roll(a, 1, axis=0))
        a2 = jnp.where(row == 0, halo[SUBLANES - 2:SUBLANES - 1, :],
                       jnp.where(row == 1, halo[SUBLANES - 1:SUBLANES, :], pltpu.roll(a, 2, axis=0)))
        halo_ref[j] = a[tm - SUBLANES:, :]
    else:
        t_in_seq = row % SUBLANES
        a1 = jnp.where(t_in_seq == 0, p1_ref[...], pltpu.roll(a, 1, axis=0))
        a2 = jnp.where(t_in_seq <= 1, p2_ref[...], pltpu.roll(a, 2, axis=0))
    cw = cw_ref[...]
    c = cb_ref[...] + cw[0:1, :] * a2 + cw[1:2, :] * a1 + cw[2:3, :] * a
    y_ref[...] = (_gelu_tanh(c) * b).astype(y_ref.dtype)
    tail_ref[0] = a[tm - tail_ref.shape[1]:, :]


def ffn_up(x, g, w_up, conv_w, conv_b, d_ff, t_seq, prev1=None, prev2=None, tm_cap=512, tn_cap=1024):
    m, k = x.shape
    assert conv_w.shape[0] == 3
    tn = _tile(d_ff, tn_cap)
    nj = d_ff // tn
    if prev1 is None:
        tm = _tile(t_seq, tm_cap, SUBLANES)
        tiles_per_seq = t_seq // tm
        tail_rows = SUBLANES
        prev1 = prev2 = jnp.zeros((SUBLANES, LANE), F32)
        prev_spec = pl.BlockSpec((SUBLANES, LANE), lambda i, j: (0, 0))
    else:
        assert t_seq == SUBLANES
        tm = _tile(m, tm_cap, SUBLANES)
        tiles_per_seq = 0
        tail_rows = tm
        prev_spec = pl.BlockSpec((tm, tn), lambda i, j: (i, j))
    return pl.pallas_call(
        functools.partial(_ffn_up_body, tiles_per_seq),
        grid=(m // tm, nj),
        in_specs=[pl.BlockSpec((tm, k), lambda i, j: (i, 0)),
                  pl.BlockSpec((1, k), lambda i, j: (0, 0)),
                  pl.BlockSpec((k, tn), lambda i, j: (0, j)),
                  pl.BlockSpec((k, tn), lambda i, j: (0, nj + j)),
                  pl.BlockSpec((3, tn), lambda i, j: (0, j)),
                  pl.BlockSpec((1, tn), lambda i, j: (0, j)),
                  prev_spec, prev_spec],
        out_specs=[pl.BlockSpec((tm, tn), lambda i, j: (i, j)),
                   pl.BlockSpec((1, tail_rows, tn), lambda i, j: (i, 0, j))],
        out_shape=[jax.ShapeDtypeStruct((m, d_ff), BF16), jax.ShapeDtypeStruct((m // tm, tail_rows, d_ff), F32)],
        scratch_shapes=[pltpu.VMEM((tm, k), BF16), pltpu.VMEM((nj, SUBLANES, tn), F32)],
        compiler_params=_params(2),
        name="ffn_up",
    )(x, g.reshape(1, k).astype(F32), w_up, w_up, conv_w.astype(F32), conv_b.reshape(1, d_ff).astype(F32), prev1, prev2)


def _trunk(x, p, layer, dims, paged, hgrn_state, mem_k, mem_v, conv_buf):
    B, T, D = x.shape
    kvw, ih, idd, hgh, d_ff = dims["kvw"], dims["ih"], dims["idd"], dims["hgh"], dims["d_ff"]
    x2 = x.reshape(B * T, D)
    seg_widths = (D, 2 * kvw, ih * idd, 4 * D, 2 * D)
    tn = _tile(math.gcd(*seg_widths), 1024)
    q2, kv2, iq2, zb, zc, side = mix_in_proj(x2, p["g_pre_mix"], p["w_main"], p["w_side"], seg_widths,
                                             (BF16, F32, BF16, F32, F32), tn)
    k = kv2[:, :kvw].reshape(B, T, -1, dims["hd"])
    v = kv2[:, kvw:].reshape(B, T, -1, dims["hd"])
    ik = side[:, :idd].reshape(B, T, idd)
    prompt = paged is None
    cfg = dict(hd=dims["hd"], kvh=kvw // dims["hd"], ih=ih, idd=idd)
    if prompt:
        o_a = dsa_prompt(q2, kv2.astype(BF16), iq2, side, p["rel_bias"], B, T, cfg)
        o_b, s_new = hgrn_prompt(zb, p["hg_lb"], layer, p["hg_norm"], B, T, hgh)
    else:
        o_a = dsa_sample(q2, kv2, iq2, side, layer=layer, rel_bias=p["rel_bias"], cfg=cfg, **paged)
        o_b, s_new = hgrn_sample(zb, p["hg_lb"], layer, p["hg_norm"], hgrn_state, T)
    x2 = merge_out_proj(o_a, o_b, zc, p["w_out"], p["g_post_mix"], x2)
    qx = norm_matmul(x2, p["g_pre_x"], p["w_xq"], BF16 if prompt else F32)
    ox = cross_attend(qx, mem_k, mem_v, 0 if prompt else layer, B, T, BF16 if prompt else F32)
    x2 = matmul_norm_res(ox, p["w_xo"], p["g_post_x"], x2)
    if prompt:
        y, tail = ffn_up(x2, p["g_pre_ffn"], p["w_up"], p["conv_w"], p["conv_b"], d_ff, T)
        tiles_per_seq = tail.shape[0] // B
        new_buf = tail[tiles_per_seq - 1::tiles_per_seq, SUBLANES - 2:, :]
    else:
        zeros = jnp.zeros((B, T - 2, d_ff), F32)
        prev1 = jnp.concatenate([conv_buf[:, 1:2], zeros, jnp.zeros((B, 1, d_ff), F32)], axis=1).reshape(B * T, d_ff)
        prev2 = jnp.concatenate([conv_buf, zeros], axis=1).reshape(B * T, d_ff)
        y, tail = ffn_up(x2, p["g_pre_ffn"], p["w_up"], p["conv_w"], p["conv_b"], d_ff, T, prev1, prev2)
        new_buf = tail.reshape(B, T, d_ff)[:, T - 2:, :]
    x2 = matmul_norm_res(y, p["w_down"], p["g_post_ffn"], x2)
    return x2.reshape(B, T, D), k, v, ik, s_new, new_buf


def kernel(x_prompt, x_sample, mem_prompt, cache_k, cache_v, cache_idx_k, cache_mem_k, cache_mem_v, state_hgrn, state_conv, page_table, rel_bias, hg_lb, g_pre_mix, w_in, hg_norm, w_out, g_post_mix, g_pre_x, g_mem, w_xq, w_xk, w_xv, w_xo, g_post_x, g_pre_ffn, w_up, conv_w, conv_b, w_down, g_post_ffn):
    depth = w_in.shape[0]
    B, S, D = x_prompt.shape
    kvh, hd = cache_k.shape[-2:]
    kvw = kvh * hd
    idd = cache_idx_k.shape[-1]
    mix = w_in.shape[-1]
    ih = (mix - 7 * D - 2 * kvw - idd) // (idd + 1)
    hgh = state_hgrn.shape[2]
    d_ff = w_down.shape[1]
    dims = dict(kvw=kvw, ih=ih, idd=idd, hgh=hgh, d_ff=d_ff, hd=hd)
    xp, xs = x_prompt, x_sample
    new = [[] for _ in range(12)]
    n_a = D + 2 * kvw + ih * idd
    pad = (-(ih + idd)) % LANE
    for l in range(depth):
        wl = w_in[l]
        w_main = jnp.concatenate([wl[:, :n_a], wl[:, n_a + ih + idd:]], axis=1).astype(BF16)
        w_side = jnp.concatenate([wl[:, n_a + ih:n_a + ih + idd], wl[:, n_a:n_a + ih], jnp.zeros((D, pad), F32)],
                                 axis=1).astype(BF16)
        p = dict(g_pre_mix=g_pre_mix[l], w_main=w_main, w_side=w_side, rel_bias=rel_bias, hg_lb=hg_lb, hg_norm=hg_norm[l],
                 w_out=w_out[l].astype(BF16),
                 g_post_mix=g_post_mix[l], g_pre_x=g_pre_x[l], w_xq=w_xq[l].astype(BF16), w_xo=w_xo[l].astype(BF16),
                 g_post_x=g_post_x[l], g_pre_ffn=g_pre_ffn[l], w_up=w_up[l].astype(BF16), conv_w=conv_w[l],
                 conv_b=conv_b[l], w_down=w_down[l].astype(BF16), g_post_ffn=g_post_ffn[l])
        mb, mm, _ = mem_prompt.shape
        xh, xd = cache_mem_k.shape[-2:]
        mem2 = mem_prompt.reshape(mb * mm, D)
        mk = norm_matmul(mem2, g_mem[l], w_xk[l].astype(BF16), F32).reshape(1, mb, mm, xh, xd)
        mv = norm_matmul(mem2, g_mem[l], w_xv[l].astype(BF16), F32).reshape(1, mb, mm, xh, xd)
        xp, kp, vp, ikp, sp, cp = _trunk(xp, p, l, dims, None, None, mk, mv, None)
        paged = dict(cache_k=cache_k, cache_v=cache_v, cache_idx_k=cache_idx_k, page_table=page_table)
        xs, ks, vs, iks, ss, cs = _trunk(xs, p, l, dims, paged, state_hgrn[l], cache_mem_k, cache_mem_v, state_conv[l])
        for store, val in zip(new, (kp, vp, ikp, ks, vs, iks, mk[0], mv[0], sp, ss, cp, cs)):
            store.append(val)
    stacked = [jnp.stack(s, axis=0) for s in new]
    return (xp, xs, *stacked)
```

```python
import functools
import math

import numpy as np
import jax
import jax.numpy as jnp
from jax import lax
from jax.experimental import pallas as pl
from jax.experimental.pallas import tpu as pltpu

F32 = jnp.float32
BF16 = jnp.bfloat16

EPS = 1e-6
TOPK_MAX = 256
MAX_DISTANCE = 128

V7X_VMEM_BYTES = 64 * 1024 * 1024
VMEM_LIMIT_BYTES = V7X_VMEM_BYTES - 8 * 1024 * 1024
LANE = 128


def _params(n_axes):
    return pltpu.CompilerParams(dimension_semantics=("arbitrary",) * n_axes, vmem_limit_bytes=VMEM_LIMIT_BYTES)


def _tile(n, cap, unit=LANE):
    if n <= cap:
        return n
    best = None
    for t in range(unit, cap + 1, unit):
        if n % t == 0:
            best = t
    assert best is not None, (n, cap, unit)
    return best


def _norm_matmul_body(x_ref, g_ref, w_ref, o_ref, h_ref):
    @pl.when(pl.program_id(1) == 0)
    def _():
        x = x_ref[...]
        ms = jnp.mean(x * x, axis=-1, keepdims=True)
        h_ref[...] = (x * lax.rsqrt(ms + EPS) * g_ref[...]).astype(BF16)

    o_ref[...] = jnp.dot(h_ref[...], w_ref[...], preferred_element_type=F32).astype(o_ref.dtype)


def norm_matmul(x, g, w, out_dtype, tm_cap=512, tn_cap=1024):
    m, k = x.shape
    n = w.shape[1]
    tm, tn = _tile(m, tm_cap, 8), _tile(n, tn_cap)
    return pl.pallas_call(
        _norm_matmul_body,
        grid=(m // tm, n // tn),
        in_specs=[pl.BlockSpec((tm, k), lambda i, j: (i, 0)),
                  pl.BlockSpec((1, k), lambda i, j: (0, 0)),
                  pl.BlockSpec((k, tn), lambda i, j: (0, j))],
        out_specs=pl.BlockSpec((tm, tn), lambda i, j: (i, j)),
        out_shape=jax.ShapeDtypeStruct((m, n), out_dtype),
        scratch_shapes=[pltpu.VMEM((tm, k), BF16)],
        compiler_params=_params(2),
        name="norm_matmul",
    )(x, g.reshape(1, k).astype(F32), w)


def _matmul_norm_res_body(a_ref, w_ref, g_ref, r_ref, o_ref, acc_ref):
    kk = pl.program_id(1)

    @pl.when(kk == 0)
    def _():
        acc_ref[...] = jnp.zeros_like(acc_ref)

    acc_ref[...] += jnp.dot(a_ref[...].astype(BF16), w_ref[...], preferred_element_type=F32)

    @pl.when(kk == pl.num_programs(1) - 1)
    def _():
        y = acc_ref[...]
        ms = jnp.mean(y * y, axis=-1, keepdims=True)
        o_ref[...] = r_ref[...] + y * lax.rsqrt(ms + EPS) * g_ref[...]


def matmul_norm_res(a, w, g, res, tm_cap=512, tk_cap=2048):
    m, k = a.shape
    n = w.shape[1]
    tm, tk = _tile(m, tm_cap, 8), _tile(k, tk_cap)
    return pl.pallas_call(
        _matmul_norm_res_body,
        grid=(m // tm, k // tk),
        in_specs=[pl.BlockSpec((tm, tk), lambda i, j: (i, j)),
                  pl.BlockSpec((tk, n), lambda i, j: (j, 0)),
                  pl.BlockSpec((1, n), lambda i, j: (0, 0)),
                  pl.BlockSpec((tm, n), lambda i, j: (i, 0))],
        out_specs=pl.BlockSpec((tm, n), lambda i, j: (i, 0)),
        out_shape=jax.ShapeDtypeStruct((m, n), F32),
        scratch_shapes=[pltpu.VMEM((tm, n), F32)],
        compiler_params=_params(2),
        name="matmul_norm_res",
    )(a, w, g.reshape(1, n).astype(F32), res)


def _mix_in_body(seg_tiles, x_ref, g_ref, w_ref, ws_ref, *rest):
    n_seg = len(seg_tiles)
    out_refs, side_ref, h_ref = rest[:n_seg], rest[n_seg], rest[n_seg + 1]
    j = pl.program_id(1)

    @pl.when(j == 0)
    def _():
        x = x_ref[...]
        ms = jnp.mean(x * x, axis=-1, keepdims=True)
        h = (x * lax.rsqrt(ms + EPS) * g_ref[...]).astype(BF16)
        h_ref[...] = h
        side_ref[...] = jnp.dot(h, ws_ref[...], preferred_element_type=F32)

    start = 0
    for o_ref, n_t in zip(out_refs, seg_tiles):
        @pl.when((j >= start) & (j < start + n_t))
        def _(o_ref=o_ref):
            o_ref[...] = jnp.dot(h_ref[...], w_ref[...], preferred_element_type=F32).astype(o_ref.dtype)
        start += n_t


def mix_in_proj(x, g, w_main, w_side, seg_widths, seg_dtypes, tn, tm_cap=1024):
    m, k = x.shape
    tm = _tile(m, tm_cap, 8)
    seg_tiles = tuple(wd // tn for wd in seg_widths)
    starts = tuple(int(s) for s in np.cumsum((0,) + seg_tiles[:-1]))

    def seg_map(start, n_t):
        return lambda i, j: (i, jnp.clip(j - start, 0, n_t - 1))

    out_specs = [pl.BlockSpec((tm, tn), seg_map(s, n)) for s, n in zip(starts, seg_tiles)]
    out_specs.append(pl.BlockSpec((tm, LANE), lambda i, j: (i, 0)))
    out_shape = [jax.ShapeDtypeStruct((m, wd), dt) for wd, dt in zip(seg_widths, seg_dtypes)]
    out_shape.append(jax.ShapeDtypeStruct((m, LANE), F32))
    return pl.pallas_call(
        functools.partial(_mix_in_body, seg_tiles),
        grid=(m // tm, sum(seg_tiles)),
        in_specs=[pl.BlockSpec((tm, k), lambda i, j: (i, 0)),
                  pl.BlockSpec((1, k), lambda i, j: (0, 0)),
                  pl.BlockSpec((k, tn), lambda i, j: (0, j)),
                  pl.BlockSpec((k, LANE), lambda i, j: (0, 0))],
        out_specs=out_specs,
        out_shape=out_shape,
        scratch_shapes=[pltpu.VMEM((tm, k), BF16)],
        compiler_params=_params(2),
        name="mix_in_proj",
    )(x, g.reshape(1, k).astype(F32), w_main, w_side)


INT_MIN = -2 ** 31
MASK_NEG = -1e30
_NT = (((1,), (1,)), ((), ()))


def _sortable_key(x):
    i = pltpu.bitcast(x, jnp.int32)
    return i ^ (lax.shift_right_arithmetic(i, 31) & jnp.int32(0x7FFFFFFF))


def _bucket_thresholds(n_buckets, max_distance, d_max):
    max_exact = n_buckets // 2
    d = np.arange(d_max, dtype=np.int64)
    log_ratio = np.log(np.maximum(d, 1).astype(np.float32) / np.float32(max_exact)) / np.float32(math.log(max_distance / max_exact))
    large = np.minimum(max_exact + (log_ratio * np.float32(n_buckets - max_exact)).astype(np.int32), n_buckets - 1)
    bucket = np.where(d < max_exact, d, large)
    assert np.all(np.diff(bucket) >= 0)
    ths = [int(np.argmax(bucket >= max_exact + n)) for n in range(1, n_buckets - max_exact)]
    return max_exact, ths, bucket


def _kth_largest_threshold(count_ge, kf, shape):
    lo = jnp.where(count_ge(jnp.zeros(shape, jnp.int32)) >= kf, jnp.int32(0), jnp.int32(INT_MIN))

    def step(bi, lo):
        cand = lo + lax.shift_left(jnp.int32(1), 30 - bi)
        return jnp.where(count_ge(cand) >= kf, cand, lo)

    return lax.fori_loop(0, 31, step, lo)


def _dsa_prompt_body(cfg, relb_ref, q_ref, kv_ref, iq_ref, sq_ref, sk_ref, o_ref,
                     ikl_scr, ikh_scr, keys_scr, madd_scr, m_scr, l_scr, acc_scr, bias_scr):
    tq, hd, kvh, qpk, ih, idd, topk, n_buckets = (cfg[k] for k in ("tq", "hd", "kvh", "qpk", "ih", "idd", "topk", "n_buckets"))
    kvw = kvh * hd
    nl = tq // LANE
    rows = qpk * tq
    scale = hd ** -0.5
    b, i = pl.program_id(0), pl.program_id(1)

    @pl.when((b == 0) & (i == 0))
    def _():
        max_exact, ths, _ = _bucket_thresholds(n_buckets, MAX_DISTANCE, 2 * tq)
        tl = lax.broadcasted_iota(jnp.int32, (tq, tq), 0)
        sl = lax.broadcasted_iota(jnp.int32, (tq, tq), 1)
        for r in range(2):
            d = jnp.maximum(r * tq + tl - sl, 0)
            bucket = jnp.minimum(d, max_exact)
            for th in ths:
                bucket = bucket + jnp.where(d >= th, 1, 0)
            for h in range(kvh * qpk):
                far = relb_ref[n_buckets - 1, h]
                val = jnp.full((tq, tq), far, F32)
                for bk in range(n_buckets - 1):
                    val = jnp.where(bucket == bk, relb_ref[bk, h], val)
                g, hl = divmod(h, qpk)
                bias_scr[g, r, hl * tq:(hl + 1) * tq, :] = (val - far) * (1.0 / scale)

    @pl.when(i == 0)
    def _():
        side = sk_ref[...]
        lane = lax.broadcasted_iota(jnp.int32, side.shape, 1)
        lo = jnp.where(lane < idd, side, 0.0)
        ikl_scr[...] = lo.astype(BF16)
        ikh_scr[...] = pltpu.roll(lo, idd, axis=1).astype(BF16)

    w_rows = sq_ref[...].T[idd:idd + ih, :] * (ih ** -0.5 * idd ** -0.5)
    t_pos = i * tq + lax.broadcasted_iota(jnp.int32, (tq, tq), 1)
    s_loc = lax.broadcasted_iota(jnp.int32, (tq, tq), 0)

    def index_chunk(c, carry):
        r0 = pl.multiple_of(c * tq, tq)
        kl = ikl_scr[pl.ds(r0, tq), :]
        kh = ikh_scr[pl.ds(r0, tq), :]
        acc = jnp.zeros((tq, tq), F32)
        for jp in range(ih // 2):
            iqp = iq_ref[:, jp * LANE:(jp + 1) * LANE]
            x0 = lax.dot_general(kl, iqp, _NT, preferred_element_type=F32)
            x1 = lax.dot_general(kh, iqp, _NT, preferred_element_type=F32)
            acc = acc + jnp.maximum(x0, 0.0) * w_rows[2 * jp:2 * jp + 1, :] + jnp.maximum(x1, 0.0) * w_rows[2 * jp + 1:2 * jp + 2, :]
        keys_scr[c] = jnp.where(c * tq + s_loc <= t_pos, _sortable_key(acc), jnp.int32(INT_MIN))
        return carry

    lax.fori_loop(0, i + 1, index_chunk, 0)

    def count_ge(cand):
        def chunk(c, acc):
            hit = jnp.where(keys_scr[c] >= cand, 1.0, 0.0)
            return acc + jnp.sum(hit.reshape(tq // 8, 8, tq), axis=0)
        acc = lax.fori_loop(0, i + 1, chunk, jnp.zeros((8, tq), F32))
        return jnp.sum(acc, axis=0, keepdims=True)

    thr = _kth_largest_threshold(count_ge, float(topk), (1, tq))
    thr = jnp.maximum(thr, jnp.int32(INT_MIN + 1))

    n_ge = count_ge(thr)
    has_ties = jnp.max(n_ge) > float(topk)

    @pl.when(jnp.logical_not(has_ties))
    def _():
        def mask_chunk(c, carry):
            madd_scr[c] = jnp.where(keys_scr[c] >= thr, 0.0, MASK_NEG).T
            return carry

        lax.fori_loop(0, i + 1, mask_chunk, 0)

    @pl.when(has_ties)
    def _():
        need = float(topk) - count_ge(thr + 1)
        lower_tri = jnp.where(lax.broadcasted_iota(jnp.int32, (tq, tq), 0) >= lax.broadcasted_iota(jnp.int32, (tq, tq), 1),
                              1.0, 0.0).astype(BF16)

        def mask_chunk(c, seen):
            kc = keys_scr[c]
            tie = jnp.where(kc == thr, 1.0, 0.0)
            rank = seen + jnp.dot(lower_tri, tie.astype(BF16), preferred_element_type=F32)
            take = (kc > thr) | ((kc == thr) & (rank <= need))
            madd_scr[c] = jnp.where(take, 0.0, MASK_NEG).T
            return seen + jnp.sum(tie, axis=0, keepdims=True)

        lax.fori_loop(0, i + 1, mask_chunk, jnp.zeros((1, tq), F32))

    cexp = scale * math.log2(math.e)

    def fold(x, op):
        y = x[:, :LANE]
        for t in range(1, nl):
            y = op(y, x[:, t * LANE:(t + 1) * LANE])
        return y

    m_scr[...] = jnp.full(m_scr.shape, -jnp.inf, F32)
    l_scr[...] = jnp.zeros(l_scr.shape, F32)
    acc_scr[...] = jnp.zeros(acc_scr.shape, F32)

    def attend_chunk(c, bias_r):
        r0 = pl.multiple_of(c * tq, tq)
        md = jnp.concatenate([madd_scr[c]] * qpk, axis=0)
        for g in range(kvh):
            qg = jnp.concatenate([q_ref[:, (g * qpk + hl) * hd:(g * qpk + hl + 1) * hd] for hl in range(qpk)], axis=0)
            kc = kv_ref[pl.ds(r0, tq), g * hd:(g + 1) * hd]
            z = lax.dot_general(qg, kc, _NT, preferred_element_type=F32) + md
            if bias_r is not None:
                z = z + bias_scr[g, bias_r]
            m_old = m_scr[g]
            m_new = jnp.maximum(m_old, jnp.max(fold(z, jnp.maximum), axis=1, keepdims=True))
            alpha = jnp.exp2((m_old - m_new) * cexp)
            p = jnp.exp2((z - jnp.concatenate([m_new] * nl, axis=1)) * cexp)
            m_scr[g] = m_new
            l_scr[g] = alpha * l_scr[g] + fold(p, jnp.add)
            vc = kv_ref[pl.ds(r0, tq), kvw + g * hd:kvw + (g + 1) * hd]
            pv = jnp.dot(p.astype(BF16), vc, preferred_element_type=F32)
            acc_scr[g] = jnp.concatenate([alpha] * (hd // LANE), axis=1) * acc_scr[g] + pv

    def far_chunk(c, carry):
        attend_chunk(c, None)
        return carry

    lax.fori_loop(0, jnp.maximum(i - 1, 0), far_chunk, 0)

    @pl.when(i >= 1)
    def _():
        attend_chunk(i - 1, 1)

    attend_chunk(i, 0)

    for g in range(kvh):
        o = acc_scr[g] / jnp.sum(l_scr[g], axis=1, keepdims=True)
        for hl in range(qpk):
            h = g * qpk + hl
            o_ref[:, h * hd:(h + 1) * hd] = o[hl * tq:(hl + 1) * tq, :].astype(o_ref.dtype)


def dsa_prompt(q_iq, kv, side, rel_bias, batch, seq, cfg, tq_cap=256):
    m = q_iq.shape[0]
    hd, kvh, ih, idd = cfg["hd"], cfg["kvh"], cfg["ih"], cfg["idd"]
    iq_width = ih * idd
    a_width = q_iq.shape[1] - iq_width
    n_heads = a_width // hd
    qpk = n_heads // kvh
    tq = _tile(seq, tq_cap)
    n_buckets = rel_bias.shape[0]
    assert 2 * idd == LANE and ih % 2 == 0 and idd + ih <= LANE and seq % tq == 0 and a_width % iq_width == 0
    _, _, bucket = _bucket_thresholds(n_buckets, MAX_DISTANCE, 2 * tq)
    assert bucket[tq + 1] == n_buckets - 1, "bias must be saturated two key chunks away from the diagonal"
    nq = seq // tq
    kcfg = dict(tq=tq, hd=hd, kvh=kvh, qpk=qpk, ih=ih, idd=idd, topk=min(TOPK_MAX, seq // 4), n_buckets=n_buckets)
    rows = qpk * tq
    return pl.pallas_call(
        functools.partial(_dsa_prompt_body, kcfg),
        grid=(batch, nq),
        in_specs=[pl.BlockSpec(memory_space=pltpu.SMEM),
                  pl.BlockSpec((tq, a_width), lambda b, i: (b * nq + i, 0)),
                  pl.BlockSpec((seq, kv.shape[1]), lambda b, i: (b, 0)),
                  pl.BlockSpec((tq, iq_width), lambda b, i: (b * nq + i, a_width // iq_width)),
                  pl.BlockSpec((tq, LANE), lambda b, i: (b * nq + i, 0)),
                  pl.BlockSpec((seq, LANE), lambda b, i: (b, 0))],
        out_specs=pl.BlockSpec((tq, a_width), lambda b, i: (b * nq + i, 0)),
        out_shape=jax.ShapeDtypeStruct((m, a_width), F32),
        scratch_shapes=[pltpu.VMEM((seq, LANE), BF16), pltpu.VMEM((seq, LANE), BF16),
                        pltpu.VMEM((nq, tq, tq), jnp.int32), pltpu.VMEM((nq, tq, tq), F32),
                        pltpu.VMEM((kvh, rows, LANE), F32), pltpu.VMEM((kvh, rows, LANE), F32),
                        pltpu.VMEM((kvh, rows, hd), F32), pltpu.VMEM((kvh, 2, rows, tq), F32)],
        compiler_params=_params(2),
        name="dsa_prompt",
    )(rel_bias.astype(F32), q_iq, kv, q_iq, side, side)


def _paged_specs(n_pages, block, layer):
    return [pl.BlockSpec((1, 1) + block, lambda b, pt, p=p: (layer, pt[b, p]) + (0,) * len(block)) for p in range(n_pages)]


def _dsa_sample_scores_body(cfg, pt_ref, iq_ref, iw_ref, *rest):
    n_pages, ih, idd, t_seq = (cfg[k] for k in ("n_pages", "ih", "idd", "t_seq"))
    page_refs, new_ref, keys_ref = rest[:n_pages], rest[n_pages], rest[n_pages + 1]
    iq = iq_ref[0]
    w = iw_ref[0] * (ih ** -0.5 * idd ** -0.5)
    page = new_ref.shape[1]
    t = lax.broadcasted_iota(jnp.int32, (t_seq, page), 0)
    j = lax.broadcasted_iota(jnp.int32, (t_seq, page), 1)
    for c in range(n_pages + 1):
        ikc = (page_refs[c][0, 0] if c < n_pages else new_ref[0]).astype(BF16)
        x = lax.dot_general(iq, ikc, _NT, preferred_element_type=F32)
        sc = jnp.sum((jnp.maximum(x, 0.0) * w).reshape(ih, t_seq, page), axis=0)
        key = _sortable_key(sc)
        if c == n_pages:
            key = jnp.where(j <= t, key, jnp.int32(INT_MIN))
        keys_ref[0, c] = key


def _dsa_sample_mask_body(topk, kvh, keys_ref, madd_ref):
    nb, n_chunks, t_seq, page = keys_ref.shape
    rows = nb * t_seq
    spread = jnp.where(lax.broadcasted_iota(jnp.int32, (page, page * kvh), 1) // kvh
                       == lax.broadcasted_iota(jnp.int32, (page, page * kvh), 0), 1.0, 0.0).astype(BF16)

    def store(c, take):
        taken = jnp.dot(jnp.where(take, 1.0, 0.0).astype(BF16), spread, preferred_element_type=F32)
        madd_ref[:, c] = jnp.where(taken > 0.5, 0.0, MASK_NEG).reshape(nb, t_seq, page * kvh)

    def chunk(c):
        return keys_ref[:, c].reshape(rows, page)

    def count_ge(cand):
        acc = jnp.zeros((rows, page), F32)
        for c in range(n_chunks):
            acc = acc + jnp.where(chunk(c) >= cand, 1.0, 0.0)
        return jnp.sum(acc, axis=1, keepdims=True)

    thr = _kth_largest_threshold(count_ge, float(topk), (rows, 1))
    thr = jnp.maximum(thr, jnp.int32(INT_MIN + 1))
    has_ties = jnp.max(count_ge(thr)) > float(topk)

    @pl.when(jnp.logical_not(has_ties))
    def _():
        for c in range(n_chunks):
            store(c, chunk(c) >= thr)

    @pl.when(has_ties)
    def _():
        need = float(topk) - count_ge(thr + 1)
        upper_tri = jnp.where(lax.broadcasted_iota(jnp.int32, (page, page), 0) <= lax.broadcasted_iota(jnp.int32, (page, page), 1),
                              1.0, 0.0).astype(BF16)
        seen = jnp.zeros((rows, 1), F32)
        for c in range(n_chunks):
            kc = chunk(c)
            tie = jnp.where(kc == thr, 1.0, 0.0)
            rank = seen + jnp.dot(tie.astype(BF16), upper_tri, preferred_element_type=F32)
            store(c, (kc > thr) | ((kc == thr) & (rank <= need)))
            seen = seen + jnp.sum(tie, axis=1, keepdims=True)


def _dsa_sample_attend_body(cfg, pt_ref, relb_ref, q_ref, madd_ref, *rest):
    n_pages, hd, kvh, qpk, t_seq, n_buckets = (cfg[k] for k in ("n_pages", "hd", "kvh", "qpk", "t_seq", "n_buckets"))
    k_refs, v_refs = rest[:n_pages], rest[n_pages:2 * n_pages]
    knew_ref, vnew_ref, o_ref, s_scr, bias_scr = rest[2 * n_pages:]
    n_chunks = n_pages + 1
    n_heads = kvh * qpk
    rows = n_heads * t_seq
    krows = knew_ref.shape[1]
    page = krows // kvh
    scale = hd ** -0.5

    @pl.when(pl.program_id(0) == 0)
    def _():
        max_exact, ths, _ = _bucket_thresholds(n_buckets, MAX_DISTANCE, 2 * page)
        t = lax.broadcasted_iota(jnp.int32, (t_seq, krows), 0)
        lane = lax.broadcasted_iota(jnp.int32, (t_seq, krows), 1)
        j, lane_head = lane // kvh, lane % kvh
        buckets = []
        for r in range(2):
            d = jnp.maximum(r * page + t - j, 0)
            bucket = jnp.minimum(d, max_exact)
            for th in ths:
                bucket = bucket + jnp.where(d >= th, 1, 0)
            buckets.append(bucket)
        for h in range(n_heads):
            own = jnp.where(lane_head == h // qpk, 0.0, MASK_NEG)
            bias_scr[0, h * t_seq:(h + 1) * t_seq, :] = own
            far = relb_ref[n_buckets - 1, h]
            for r in range(2):
                val = jnp.full((t_seq, krows), far, F32)
                for bk in range(n_buckets - 1):
                    val = jnp.where(buckets[r] == bk, relb_ref[bk, h], val)
                bias_scr[1 + r, h * t_seq:(h + 1) * t_seq, :] = own + (val - far) * (1.0 / scale)

    def key_rows(c, refs, new_ref):
        if c < n_pages:
            return refs[c][0, 0].reshape(krows, hd).astype(BF16)
        return new_ref[0].astype(BF16)

    q = q_ref[0]
    m = jnp.full((rows, LANE), -jnp.inf, F32)
    for c in range(n_chunks):
        z = lax.dot_general(q, key_rows(c, k_refs, knew_ref), _NT, preferred_element_type=F32)
        z = z + jnp.concatenate([madd_ref[0, c]] * n_heads, axis=0)
        z = z + bias_scr[0 if c < n_chunks - 2 else n_chunks - c]
        s_scr[c] = z
        for t0 in range(0, krows, LANE):
            m = jnp.maximum(m, z[:, t0:t0 + LANE])
    m_row = jnp.max(m, axis=1, keepdims=True)
    cexp = scale * math.log2(math.e)
    l_acc = jnp.zeros((rows, LANE), F32)
    acc = jnp.zeros((rows, hd), F32)
    for c in range(n_chunks):
        p = jnp.exp2((s_scr[c] - m_row) * cexp)
        for t0 in range(0, krows, LANE):
            l_acc = l_acc + p[:, t0:t0 + LANE]
        acc = acc + jnp.dot(p.astype(BF16), key_rows(c, v_refs, vnew_ref), preferred_element_type=F32)
    o_ref[0] = acc / jnp.sum(l_acc, axis=1, keepdims=True)


def dsa_sample(q, kv, iq, side, cache_k, cache_v, cache_idx_k, layer, page_table, rel_bias, cfg, seqs_per_mask_step=16):
    hd, kvh, ih, idd = cfg["hd"], cfg["kvh"], cfg["ih"], cfg["idd"]
    n_seq, n_pages = page_table.shape
    page = cache_k.shape[2]
    m, a_width = q.shape
    t_seq = m // n_seq
    n_heads = a_width // hd
    qpk = n_heads // kvh
    kvw = kvh * hd
    n_chunks = n_pages + 1
    n_buckets = rel_bias.shape[0]
    assert t_seq == SUBLANES and page == LANE
    _, _, bucket = _bucket_thresholds(n_buckets, MAX_DISTANCE, 2 * page)
    assert bucket[page + 1] == n_buckets - 1, "bias must be saturated two pages before the new tokens"
    topk = min(TOPK_MAX, (n_pages * page + t_seq) // 4)
    iq_ht = iq.reshape(n_seq, t_seq, ih, idd).transpose(0, 2, 1, 3).reshape(n_seq, ih * t_seq, idd)
    iw_ht = side[:, idd:idd + ih].reshape(n_seq, t_seq, ih).transpose(0, 2, 1).reshape(n_seq, ih * t_seq, 1)
    q_ht = q.reshape(n_seq, t_seq, n_heads, hd).transpose(0, 2, 1, 3).reshape(n_seq, n_heads * t_seq, hd)
    pad_rows = ((0, 0), (0, page - t_seq), (0, 0))
    ik_new = jnp.pad(side[:, :idd].reshape(n_seq, t_seq, idd), pad_rows)
    kv_new = jnp.pad(kv.reshape(n_seq, t_seq, 2 * kvw), pad_rows)
    k_new = kv_new[:, :, :kvw].reshape(n_seq, page * kvh, hd)
    v_new = kv_new[:, :, kvw:].reshape(n_seq, page * kvh, hd)
    per_seq = lambda shape: pl.BlockSpec((1,) + shape, lambda b, pt: (b,) + (0,) * len(shape))

    scfg = dict(n_pages=n_pages, ih=ih, idd=idd, t_seq=t_seq)
    keys = pl.pallas_call(
        functools.partial(_dsa_sample_scores_body, scfg),
        grid_spec=pltpu.PrefetchScalarGridSpec(
            num_scalar_prefetch=1, grid=(n_seq,),
            in_specs=[per_seq((ih * t_seq, idd)), per_seq((ih * t_seq, 1))]
                     + _paged_specs(n_pages, (page, idd), layer) + [per_seq((page, idd))],
            out_specs=per_seq((n_chunks, t_seq, page))),
        out_shape=jax.ShapeDtypeStruct((n_seq, n_chunks, t_seq, page), jnp.int32),
        compiler_params=_params(1),
        name="dsa_sample_scores",
    )(page_table, iq_ht, iw_ht, *([cache_idx_k] * n_pages), ik_new)

    nb = _tile(n_seq, seqs_per_mask_step, 1)
    madd = pl.pallas_call(
        functools.partial(_dsa_sample_mask_body, topk, kvh),
        grid=(n_seq // nb,),
        in_specs=[pl.BlockSpec((nb, n_chunks, t_seq, page), lambda i: (i, 0, 0, 0))],
        out_specs=pl.BlockSpec((nb, n_chunks, t_seq, page * kvh), lambda i: (i, 0, 0, 0)),
        out_shape=jax.ShapeDtypeStruct((n_seq, n_chunks, t_seq, page * kvh), F32),
        compiler_params=_params(1),
        name="dsa_sample_mask",
    )(keys)

    acfg = dict(n_pages=n_pages, hd=hd, kvh=kvh, qpk=qpk, t_seq=t_seq, n_buckets=n_buckets)
    o_ht = pl.pallas_call(
        functools.partial(_dsa_sample_attend_body, acfg),
        grid_spec=pltpu.PrefetchScalarGridSpec(
            num_scalar_prefetch=1, grid=(n_seq,),
            in_specs=[pl.BlockSpec(memory_space=pltpu.SMEM), per_seq((n_heads * t_seq, hd)),
                      per_seq((n_chunks, t_seq, page * kvh))]
                     + 2 * _paged_specs(n_pages, (page, kvh, hd), layer) + 2 * [per_seq((page * kvh, hd))],
            out_specs=per_seq((n_heads * t_seq, hd)),
            scratch_shapes=[pltpu.VMEM((n_chunks, n_heads * t_seq, page * kvh), F32),
                            pltpu.VMEM((3, n_heads * t_seq, page * kvh), F32)]),
        out_shape=jax.ShapeDtypeStruct((n_seq, n_heads * t_seq, hd), F32),
        compiler_params=_params(1),
        name="dsa_sample_attend",
    )(page_table, rel_bias.astype(F32), q_ht, madd, *([cache_k] * n_pages), *([cache_v] * n_pages), k_new, v_new)
    return o_ht.reshape(n_seq, n_heads, t_seq, hd).transpose(0, 2, 1, 3).reshape(m, a_width)


SUBLANES = 8
GROUP_EXP_CLIP = 40.0
_TN = (((0,), (0,)), ((), ()))


def _split3_bf16(x):
    h1 = x.astype(BF16)
    r1 = x - h1.astype(F32)
    h2 = r1.astype(BF16)
    h3 = (r1 - h2.astype(F32)).astype(BF16)
    return jnp.concatenate([h1, h2, h3], axis=1)


def _block_row(b, m, row):
    c, w = b.shape
    b3 = b.reshape(c // m, m, w)
    return jnp.broadcast_to(b3[:, row:row + 1, :], (c // m, m, w)).reshape(c, w)


def _lower_bound(lb_logits, layer):
    e = jnp.exp(lb_logits - jnp.max(lb_logits, axis=0, keepdims=True))
    sm = e / jnp.sum(e, axis=0, keepdims=True)
    return jnp.sum(sm[1:layer + 2, :], axis=0, keepdims=True)


def _hgrn_gates(hf, lb):
    f = lb + (1.0 - lb) * jax.nn.sigmoid(hf)
    return jnp.log(f), 1.0 - f


def _hgrn_intra(qs_, kks_, bs_, block, level_of_pair):
    n = len(qs_)
    c = qs_[0].shape[0]
    a = [jnp.zeros((c, c), F32) for _ in range(n)]
    m = block
    while m >= SUBLANES:
        top = 0.0 if m > SUBLANES else GROUP_EXP_CLIP
        ref_row = [_block_row(b, m, m // 2 - 1) for b in bs_]
        qs = [(qs_[i] * jnp.exp(jnp.minimum(bs_[i] - ref_row[i], top))).astype(BF16) for i in range(n)]
        ks = [(kks_[i] * jnp.exp(jnp.minimum(ref_row[i] - bs_[i], top))).astype(BF16) for i in range(n)]
        pr = [lax.dot_general(qs[i], ks[i], _NT, preferred_element_type=F32) for i in range(n)]
        a = [jnp.where(level_of_pair == m, pr[i], a[i]) for i in range(n)]
        m //= 2
    return a


def _pair_levels(c, block):
    t = lax.broadcasted_iota(jnp.int32, (c, c), 0)
    s = lax.broadcasted_iota(jnp.int32, (c, c), 1)
    x = t ^ s
    lvl = jnp.full((c, c), SUBLANES, jnp.int32)
    m = SUBLANES
    while m < block:
        lvl = jnp.where(x >= m, 2 * m, lvl)
        m *= 2
    return jnp.where((s <= t) & (x < block), lvl, 0)


def _head_norm_gate(o, gn, hg):
    y = o * lax.rsqrt(jnp.mean(o * o, axis=-1, keepdims=True) + EPS) * gn
    return y * (hg * jax.nn.sigmoid(hg))


def _hgrn_prompt_body(chunk, layer, hps, hq_ref, hf_ref, hi_ref, hg_ref, lb_ref, gn_ref, o_ref, s_ref, st_scr):
    seq = hq_ref.shape[0]
    dk = hq_ref.shape[1] // hps
    c = chunk
    lb, gn = _lower_bound(lb_ref[...], layer), gn_ref[...]
    t = lax.broadcasted_iota(jnp.int32, (c, c), 0)
    s = lax.broadcasted_iota(jnp.int32, (c, c), 1)
    cum = jnp.where(s <= t, 1.0, 0.0).astype(BF16)
    levels = _pair_levels(c, c)
    st_scr[...] = jnp.zeros_like(st_scr)

    def step(ci, carry):
        r0 = pl.multiple_of(ci * c, c)
        rows = pl.ds(r0, c)
        hs = range(hps)
        cols = [slice(hh * dk, (hh + 1) * dk) for hh in hs]
        q = [hq_ref[rows, cs] for cs in cols]
        v = [hi_ref[rows, cs].astype(BF16) for cs in cols]
        gates = [_hgrn_gates(hf_ref[rows, cs], lb[:, cs]) for cs in cols]
        b3 = [jnp.dot(cum, _split3_bf16(g[0]), preferred_element_type=F32) for g in gates]
        b = [x[:, :dk] + x[:, dk:2 * dk] + x[:, 2 * dk:] for x in b3]
        kk = [g[1] for g in gates]
        a = _hgrn_intra(q, kk, b, c, levels)
        st = [st_scr[hh] for hh in hs]
        o = [jnp.dot(a[hh].astype(BF16), v[hh], preferred_element_type=F32) for hh in hs]
        qd = [(q[hh] * jnp.exp(b[hh])).astype(BF16) for hh in hs]
        o = [o[hh] + lax.dot_general(qd[hh], st[hh].astype(BF16), _NT, preferred_element_type=F32) for hh in hs]
        b_end = [b[hh][c - 1:c, :] for hh in hs]
        kd = [(kk[hh] * jnp.exp(b_end[hh] - b[hh])).astype(BF16) for hh in hs]
        for hh in hs:
            st_scr[hh] = st[hh] * jnp.exp(b_end[hh]) + lax.dot_general(v[hh], kd[hh], _TN, preferred_element_type=F32)
        for hh in hs:
            o_ref[rows, cols[hh]] = _head_norm_gate(o[hh], gn[:, cols[hh]], hg_ref[rows, cols[hh]])
        return carry

    lax.fori_loop(0, seq // c, step, 0)
    for hh in range(hps):
        s_ref[0, hh] = st_scr[hh].T


def hgrn_prompt(hz, hg_lb, layer, hg_norm, batch, seq, heads, chunk=128, heads_per_step=4):
    m = hz.shape[0]
    d = hg_norm.shape[-1]
    dk = d // heads
    hps = heads_per_step
    assert dk == LANE and seq % chunk == 0 and heads % hps == 0
    hg_steps = heads // hps
    spec = lambda part: pl.BlockSpec((seq, hps * dk), lambda b, h: (b, part * hg_steps + h))
    vec = pl.BlockSpec((1, hps * dk), lambda b, h: (0, h))
    lb_spec = pl.BlockSpec((hg_lb.shape[0], hps * dk), lambda b, h: (0, h))
    return pl.pallas_call(
        functools.partial(_hgrn_prompt_body, chunk, layer, hps),
        grid=(batch, hg_steps),
        in_specs=[spec(0), spec(1), spec(2), spec(3), lb_spec, vec],
        out_specs=[pl.BlockSpec((seq, hps * dk), lambda b, h: (b, h)),
                   pl.BlockSpec((1, hps, dk, dk), lambda b, h: (b, h, 0, 0))],
        out_shape=[jax.ShapeDtypeStruct((m, d), F32), jax.ShapeDtypeStruct((batch, heads, dk, dk), F32)],
        scratch_shapes=[pltpu.VMEM((hps, dk, dk), F32)],
        compiler_params=_params(2),
        name="hgrn_prompt",
    )(hz, hz, hz, hz, hg_lb.astype(F32), hg_norm.reshape(1, d).astype(F32))


def _hgrn_sample_body(t_seq, layer, hq_ref, hf_ref, hi_ref, hg_ref, lb_ref, gn_ref, s0_ref, o_ref, s_ref):
    c, dk = hq_ref.shape
    nb = c // t_seq
    lb, gn = _lower_bound(lb_ref[...], layer), gn_ref[...]
    t = lax.broadcasted_iota(jnp.int32, (c, c), 0)
    s = lax.broadcasted_iota(jnp.int32, (c, c), 1)
    same_seq = (t ^ s) < t_seq
    cum = jnp.where((s <= t) & same_seq, 1.0, 0.0).astype(BF16)
    levels = _pair_levels(c, t_seq)
    q, v = hq_ref[...], hi_ref[...]
    logf, kk = _hgrn_gates(hf_ref[...], lb)
    b3 = jnp.dot(cum, _split3_bf16(logf), preferred_element_type=F32)
    b = b3[:, :dk] + b3[:, dk:2 * dk] + b3[:, 2 * dk:]
    a = _hgrn_intra([q], [kk], [b], t_seq, levels)[0]
    o = jnp.dot(a.astype(BF16), v.astype(BF16), preferred_element_type=F32)
    qd = (q * jnp.exp(b)).astype(BF16)
    b_end_rows = _block_row(b, t_seq, t_seq - 1)
    kd = (kk * jnp.exp(b_end_rows - b)).astype(BF16)
    row = lax.broadcasted_iota(jnp.int32, (c, dk), 0)
    for n in range(nb):
        in_seq = (row >= n * t_seq) & (row < (n + 1) * t_seq)
        st = s0_ref[n, 0].T
        o = o + jnp.where(in_seq, lax.dot_general(qd, st.astype(BF16), _NT, preferred_element_type=F32), 0.0)
        b_end = b[(n + 1) * t_seq - 1:(n + 1) * t_seq, :]
        vn = jnp.where(in_seq, v, 0.0).astype(BF16)
        st_new = st * jnp.exp(b_end) + lax.dot_general(vn, kd, _TN, preferred_element_type=F32)
        s_ref[n, 0] = st_new.T
    o_ref[...] = _head_norm_gate(o, gn, hg_ref[...])


def hgrn_sample(hz, hg_lb, layer, hg_norm, state, t_seq, rows_per_step=128):
    m = hz.shape[0]
    d = hg_norm.shape[-1]
    n_seq, heads, dk, dv = state.shape
    assert dk == LANE and dv == LANE and t_seq == SUBLANES and m == n_seq * t_seq
    c = _tile(m, rows_per_step, t_seq)
    nb = c // t_seq
    spec = lambda part: pl.BlockSpec((c, dk), lambda i, h: (i, part * heads + h))
    vec = pl.BlockSpec((1, dk), lambda i, h: (0, h))
    st_spec = pl.BlockSpec((nb, 1, dk, dv), lambda i, h: (i, h, 0, 0))
    lb_spec = pl.BlockSpec((hg_lb.shape[0], dk), lambda i, h: (0, h))
    return pl.pallas_call(
        functools.partial(_hgrn_sample_body, t_seq, layer),
        grid=(m // c, heads),
        in_specs=[spec(0), spec(1), spec(2), spec(3), lb_spec, vec, st_spec],
        out_specs=[pl.BlockSpec((c, dk), lambda i, h: (i, h)), st_spec],
        out_shape=[jax.ShapeDtypeStruct((m, d), F32), jax.ShapeDtypeStruct(state.shape, F32)],
        compiler_params=_params(2),
        name="hgrn_sample",
    )(hz, hz, hz, hz, hg_lb.astype(F32), hg_norm.reshape(1, d).astype(F32), state)


def _merge_out_body(oa_ref, ob_ref, ga_ref, gb_ref, w_ref, g_ref, r_ref, o_ref):
    u = jax.nn.sigmoid(ga_ref[...]) * oa_ref[...] + jax.nn.sigmoid(gb_ref[...]) * ob_ref[...]
    y = jnp.dot(u.astype(BF16), w_ref[...], preferred_element_type=F32)
    ms = jnp.mean(y * y, axis=-1, keepdims=True)
    o_ref[...] = r_ref[...] + y * lax.rsqrt(ms + EPS) * g_ref[...]


def merge_out_proj(o_a, o_b, gz, gate_block, w, g, res, tm_cap=256):
    m, d = o_a.shape
    n = w.shape[1]
    tm = _tile(m, tm_cap, 8)
    row = lambda i: (i, 0)
    return pl.pallas_call(
        _merge_out_body,
        grid=(m // tm,),
        in_specs=[pl.BlockSpec((tm, d), row), pl.BlockSpec((tm, d), row),
                  pl.BlockSpec((tm, d), lambda i: (i, gate_block)), pl.BlockSpec((tm, d), lambda i: (i, gate_block + 1)),
                  pl.BlockSpec((d, n), lambda i: (0, 0)), pl.BlockSpec((1, n), lambda i: (0, 0)),
                  pl.BlockSpec((tm, n), row)],
        out_specs=pl.BlockSpec((tm, n), row),
        out_shape=jax.ShapeDtypeStruct((m, n), F32),
        compiler_params=_params(1),
        name="merge_out_proj",
    )(o_a, o_b, gz, gz, w, g.reshape(1, n).astype(F32), res)


def _xattn_body(xh, q_ref, mk_ref, mv_ref, o_ref):
    tq, d = q_ref.shape
    xd = d // xh
    rows_m = mk_ref.shape[2] * xh
    q = jnp.concatenate([q_ref[:, h * xd:(h + 1) * xd] for h in range(xh)], axis=0).astype(BF16)
    mk = mk_ref[0, 0].reshape(rows_m, xd).astype(BF16)
    mv = mv_ref[0, 0].reshape(rows_m, xd).astype(BF16)
    s = lax.dot_general(q, mk, _NT, preferred_element_type=F32) * (xd ** -0.5)
    row_head = lax.broadcasted_iota(jnp.int32, (xh * tq, rows_m), 0) // tq
    col_head = lax.broadcasted_iota(jnp.int32, (xh * tq, rows_m), 1) % xh
    s = jnp.where(row_head == col_head, s, MASK_NEG)
    p = jnp.exp(s - jnp.max(s, axis=-1, keepdims=True))
    l_row = jnp.sum(p, axis=-1, keepdims=True)
    o = jnp.dot(p.astype(BF16), mv, preferred_element_type=F32) / l_row
    for h in range(xh):
        o_ref[:, h * xd:(h + 1) * xd] = o[h * tq:(h + 1) * tq, :].astype(o_ref.dtype)


def cross_attend(q, mem_k, mem_v, layer, batch, t_seq, out_dtype, tq_cap=256):
    m, d = q.shape
    mlen, xh, xd = mem_k.shape[2:]
    tq = _tile(t_seq, tq_cap, 8)
    nq = t_seq // tq
    mem_spec = pl.BlockSpec((1, 1, mlen, xh, xd), lambda b, i: (layer, b, 0, 0, 0))
    return pl.pallas_call(
        functools.partial(_xattn_body, xh),
        grid=(batch, nq),
        in_specs=[pl.BlockSpec((tq, d), lambda b, i: (b * nq + i, 0)), mem_spec, mem_spec],
        out_specs=pl.BlockSpec((tq, d), lambda b, i: (b * nq + i, 0)),
        out_shape=jax.ShapeDtypeStruct((m, d), out_dtype),
        compiler_params=_params(2),
        name="cross_attend",
    )(q, mem_k, mem_v)


def _gelu_tanh(x):
    return 0.5 * x * (1.0 + jnp.tanh(math.sqrt(2.0 / math.pi) * (x + 0.044715 * x * x * x)))


def _ffn_up_body(tiles_per_seq, col_slab, x_ref, g_ref, wa_ref, wb_ref, cw_ref, cb_ref, p1_ref, p2_ref, y_ref, tail_ref, h_ref, halo_ref):
    i, j = pl.program_id(0), pl.program_id(1)
    tm, tn = y_ref.shape

    @pl.when(j == 0)
    def _():
        x = x_ref[...]
        ms = jnp.mean(x * x, axis=-1, keepdims=True)
        h_ref[...] = (x * lax.rsqrt(ms + EPS) * g_ref[...]).astype(BF16)

    if tiles_per_seq:
        @pl.when(i % tiles_per_seq == 0)
        def _():
            halo_ref[j] = jnp.zeros((SUBLANES, tn), F32)

    h = h_ref[...]
    tail_rows = tail_ref.shape[1]
    for c0 in range(0, tn, col_slab):
        cs = slice(c0, c0 + col_slab)
        a = jnp.dot(h, wa_ref[:, cs], preferred_element_type=F32)
        b = jnp.dot(h, wb_ref[:, cs], preferred_element_type=F32)
        row = lax.broadcasted_iota(jnp.int32, (tm, col_slab), 0)
        if tiles_per_seq:
            halo = halo_ref[j, :, cs]
            a1 = jnp.where(row == 0, halo[SUBLANES - 1:SUBLANES, :], pltpu.roll(a, 1, axis=0))
            a2 = jnp.where(row == 0, halo[SUBLANES - 2:SUBLANES - 1, :],
                           jnp.where(row == 1, halo[SUBLANES - 1:SUBLANES, :], pltpu.roll(a, 2, axis=0)))
            halo_ref[j, :, cs] = a[tm - SUBLANES:, :]
        else:
            t_in_seq = row % SUBLANES
            a1 = jnp.where(t_in_seq == 0, p1_ref[:, cs], pltpu.roll(a, 1, axis=0))
            a2 = jnp.where(t_in_seq <= 1, p2_ref[:, cs], pltpu.roll(a, 2, axis=0))
        cw = cw_ref[:, cs]
        c = cb_ref[:, cs] + cw[0:1, :] * a2 + cw[1:2, :] * a1 + cw[2:3, :] * a
        y_ref[:, cs] = (_gelu_tanh(c) * b).astype(y_ref.dtype)
        tail_ref[0, :, cs] = a[tm - tail_rows:, :]


def ffn_up(x, g, w_up, conv_w, conv_b, d_ff, t_seq, prev1=None, prev2=None, tm_cap=1024, tn_cap=1024, col_slab=256):
    m, k = x.shape
    assert conv_w.shape[0] == 3
    tn = _tile(d_ff, tn_cap)
    nj = d_ff // tn
    col_slab = math.gcd(tn, col_slab)
    if prev1 is None:
        tm = _tile(t_seq, tm_cap, SUBLANES)
        tiles_per_seq = t_seq // tm
        tail_rows = SUBLANES
        prev1 = prev2 = jnp.zeros((SUBLANES, LANE), F32)
        prev_spec = pl.BlockSpec((SUBLANES, LANE), lambda i, j: (0, 0))
    else:
        assert t_seq == SUBLANES
        tm = _tile(m, tm_cap, SUBLANES)
        tiles_per_seq = 0
        tail_rows = tm
        prev_spec = pl.BlockSpec((tm, tn), lambda i, j: (i, j))
    return pl.pallas_call(
        functools.partial(_ffn_up_body, tiles_per_seq, col_slab),
        grid=(m // tm, nj),
        in_specs=[pl.BlockSpec((tm, k), lambda i, j: (i, 0)),
                  pl.BlockSpec((1, k), lambda i, j: (0, 0)),
                  pl.BlockSpec((k, tn), lambda i, j: (0, j)),
                  pl.BlockSpec((k, tn), lambda i, j: (0, nj + j)),
                  pl.BlockSpec((3, tn), lambda i, j: (0, j)),
                  pl.BlockSpec((1, tn), lambda i, j: (0, j)),
                  prev_spec, prev_spec],
        out_specs=[pl.BlockSpec((tm, tn), lambda i, j: (i, j)),
                   pl.BlockSpec((1, tail_rows, tn), lambda i, j: (i, 0, j))],
        out_shape=[jax.ShapeDtypeStruct((m, d_ff), BF16), jax.ShapeDtypeStruct((m // tm, tail_rows, d_ff), F32)],
        scratch_shapes=[pltpu.VMEM((tm, k), BF16), pltpu.VMEM((nj, SUBLANES, tn), F32)],
        compiler_params=_params(2),
        name="ffn_up",
    )(x, g.reshape(1, k).astype(F32), w_up, w_up, conv_w.astype(F32), conv_b.reshape(1, d_ff).astype(F32), prev1, prev2)


def _trunk(x, p, layer, dims, paged, hgrn_state, mem_k, mem_v, conv_buf):
    B, T, D = x.shape
    kvw, ih, idd, hgh, d_ff = dims["kvw"], dims["ih"], dims["idd"], dims["hgh"], dims["d_ff"]
    x2 = x.reshape(B * T, D)
    seg_widths = (D + ih * idd, 6 * D + 2 * kvw)
    tn = _tile(math.gcd(*seg_widths), 1024)
    z16, z32, side = mix_in_proj(x2, p["g_pre_mix"], p["w_main"], p["w_side"], seg_widths, (BF16, F32), tn)
    kv2 = z32[:, 6 * D:]
    k = kv2[:, :kvw].reshape(B, T, -1, dims["hd"])
    v = kv2[:, kvw:].reshape(B, T, -1, dims["hd"])
    ik = side[:, :idd].reshape(B, T, idd)
    prompt = paged is None
    cfg = dict(hd=dims["hd"], kvh=kvw // dims["hd"], ih=ih, idd=idd)
    if prompt:
        o_a = dsa_prompt(z16, kv2.astype(BF16), side, p["rel_bias"], B, T, cfg)
        o_b, s_new = hgrn_prompt(z32, p["hg_lb"], layer, p["hg_norm"], B, T, hgh)
    else:
        o_a = dsa_sample(z16[:, :D], kv2, z16[:, D:], side, layer=layer, rel_bias=p["rel_bias"], cfg=cfg, **paged)
        o_b, s_new = hgrn_sample(z32, p["hg_lb"], layer, p["hg_norm"], hgrn_state, T)
    x2 = merge_out_proj(o_a, o_b, z32, 4, p["w_out"], p["g_post_mix"], x2)
    qx = norm_matmul(x2, p["g_pre_x"], p["w_xq"], BF16 if prompt else F32)
    ox = cross_attend(qx, mem_k, mem_v, 0 if prompt else layer, B, T, BF16 if prompt else F32)
    x2 = matmul_norm_res(ox, p["w_xo"], p["g_post_x"], x2)
    if prompt:
        y, tail = ffn_up(x2, p["g_pre_ffn"], p["w_up"], p["conv_w"], p["conv_b"], d_ff, T)
        tiles_per_seq = tail.shape[0] // B
        new_buf = tail[tiles_per_seq - 1::tiles_per_seq, SUBLANES - 2:, :]
    else:
        zeros = jnp.zeros((B, T - 2, d_ff), F32)
        prev1 = jnp.concatenate([conv_buf[:, 1:2], zeros, jnp.zeros((B, 1, d_ff), F32)], axis=1).reshape(B * T, d_ff)
        prev2 = jnp.concatenate([conv_buf, zeros], axis=1).reshape(B * T, d_ff)
        y, tail = ffn_up(x2, p["g_pre_ffn"], p["w_up"], p["conv_w"], p["conv_b"], d_ff, T, prev1, prev2)
        new_buf = tail.reshape(B, T, d_ff)[:, T - 2:, :]
    x2 = matmul_norm_res(y, p["w_down"], p["g_post_ffn"], x2)
    return x2.reshape(B, T, D), k, v, ik, s_new, new_buf


def kernel(x_prompt, x_sample, mem_prompt, cache_k, cache_v, cache_idx_k, cache_mem_k, cache_mem_v, state_hgrn, state_conv, page_table, rel_bias, hg_lb, g_pre_mix, w_in, hg_norm, w_out, g_post_mix, g_pre_x, g_mem, w_xq, w_xk, w_xv, w_xo, g_post_x, g_pre_ffn, w_up, conv_w, conv_b, w_down, g_post_ffn):
    depth = w_in.shape[0]
    B, S, D = x_prompt.shape
    kvh, hd = cache_k.shape[-2:]
    kvw = kvh * hd
    idd = cache_idx_k.shape[-1]
    mix = w_in.shape[-1]
    ih = (mix - 7 * D - 2 * kvw - idd) // (idd + 1)
    hgh = state_hgrn.shape[2]
    d_ff = w_down.shape[1]
    dims = dict(kvw=kvw, ih=ih, idd=idd, hgh=hgh, d_ff=d_ff, hd=hd)
    xp, xs = x_prompt, x_sample
    new = [[] for _ in range(12)]
    n_a = D + 2 * kvw + ih * idd
    pad = (-(ih + idd)) % LANE
    for l in range(depth):
        wl = w_in[l]
        w_main = jnp.concatenate([wl[:, :D], wl[:, D + 2 * kvw:n_a], wl[:, n_a + ih + idd:], wl[:, D:D + 2 * kvw]],
                                 axis=1).astype(BF16)
        w_side = jnp.concatenate([wl[:, n_a + ih:n_a + ih + idd], wl[:, n_a:n_a + ih], jnp.zeros((D, pad), F32)],
                                 axis=1).astype(BF16)
        p = dict(g_pre_mix=g_pre_mix[l], w_main=w_main, w_side=w_side, rel_bias=rel_bias, hg_lb=hg_lb, hg_norm=hg_norm[l],
                 w_out=w_out[l].astype(BF16),
                 g_post_mix=g_post_mix[l], g_pre_x=g_pre_x[l], w_xq=w_xq[l].astype(BF16), w_xo=w_xo[l].astype(BF16),
                 g_post_x=g_post_x[l], g_pre_ffn=g_pre_ffn[l], w_up=w_up[l].astype(BF16), conv_w=conv_w[l],
                 conv_b=conv_b[l], w_down=w_down[l].astype(BF16), g_post_ffn=g_post_ffn[l])
        mb, mm, _ = mem_prompt.shape
        xh, xd = cache_mem_k.shape[-2:]
        mem2 = mem_prompt.reshape(mb * mm, D)
        mk = norm_matmul(mem2, g_mem[l], w_xk[l].astype(BF16), F32).reshape(1, mb, mm, xh, xd)
        mv = norm_matmul(mem2, g_mem[l], w_xv[l].astype(BF16), F32).reshape(1, mb, mm, xh, xd)
        xp, kp, vp, ikp, sp, cp = _trunk(xp, p, l, dims, None, None, mk, mv, None)
        paged = dict(cache_k=cache_k, cache_v=cache_v, cache_idx_k=cache_idx_k, page_table=page_table)
        xs, ks, vs, iks, ss, cs = _trunk(xs, p, l, dims, paged, state_hgrn[l], cache_mem_k, cache_mem_v, state_conv[l])
        for store, val in zip(new, (kp, vp, ikp, ks, vs, iks, mk[0], mv[0], sp, ss, cp, cs)):
            store.append(val)
    stacked = [jnp.stack(s, axis=0) for s in new]
    return (xp, xs, *stacked)
```
